```python
import jax
import jax.numpy as jnp
from jax import lax
import numpy as np

D_MODEL = 1024
BATCH = 32
SEQ = 2048
DEPTH = 1
DEC_BATCH = 32
DEC_SEQ = 32
PAST_LEN = 4096

CHUNK = 64
N_META = 16
Q_BLOCK = 128
EPS = 1e-6
NEG = -1e30
MLA_HEADS = 8
QK_NOPE = 64
QK_ROPE = 32
V_DIM = 64
Q_LORA = 256
KV_LORA = 128
ROPE_BASE = 10000.0
ATTN_SCALE = (QK_NOPE + QK_ROPE) ** -0.5
MLA_DIM = MLA_HEADS * V_DIM
MLA_COLS = Q_LORA + KV_LORA + QK_ROPE
RWKV_HEADS = 8
RWKV_HEAD = 64
RWKV_DIM = RWKV_HEADS * RWKV_HEAD
DECAY_LORA = 64
AAA_LORA = 64
GATE_LORA = 128
LNX_EPS = 64e-5
RWKV_COLS = 3 * RWKV_DIM + DECAY_LORA + AAA_LORA + GATE_LORA
IN_COLS = MLA_COLS + RWKV_COLS
MIX_DIM = MLA_DIM + RWKV_DIM
D_FF = -(-8 * D_MODEL // (3 * 256)) * 256

kernel_name = 'hymba_mla_rwkv7_streaming_step'


def rmsnorm(x, g, eps=EPS):
    xf = x.astype(jnp.float32)
    y = xf * lax.rsqrt(jnp.mean(xf * xf, axis=-1, keepdims=True) + eps)
    return (y * g.astype(jnp.float32)).astype(x.dtype)


def rope(x, pos):
    half = x.shape[-1] // 2
    inv = ROPE_BASE ** (-jnp.arange(half, dtype=jnp.float32) / half)
    ang = pos.astype(jnp.float32)[:, None] * inv[None, :]
    shape = (ang.shape[0],) + (1,) * (x.ndim - 3) + (half,)
    cos = jnp.cos(ang).reshape(shape).astype(x.dtype)
    sin = jnp.sin(ang).reshape(shape).astype(x.dtype)
    x1, x2 = x[..., :half], x[..., half:]
    return jnp.concatenate([x1 * cos - x2 * sin, x1 * sin + x2 * cos], axis=-1)


def mla_kv_rows(p_mla, pos, kv_norm_g, kn_rope_g):
    c = rmsnorm(p_mla[..., Q_LORA:Q_LORA + KV_LORA], kv_norm_g)
    kr = rope(rmsnorm(p_mla[..., Q_LORA + KV_LORA:MLA_COLS], kn_rope_g), pos)
    return c, kr


def mla_queries(p_mla, pos, q_norm_g, w_q_up, qn_nope_g, qn_rope_g):
    b, l = p_mla.shape[:2]
    ql = rmsnorm(p_mla[..., :Q_LORA], q_norm_g)
    q = (ql @ w_q_up).reshape(b, l, MLA_HEADS, QK_NOPE + QK_ROPE)
    qn = rmsnorm(q[..., :QK_NOPE], qn_nope_g)
    qr = rope(rmsnorm(q[..., QK_NOPE:], qn_rope_g), pos)
    return qn, qr


def expand_latent(c, w_kv_up, kn_nope_g):
    b, l = c.shape[:2]
    kv = (c @ w_kv_up).reshape(b, l, MLA_HEADS, QK_NOPE + V_DIM)
    return rmsnorm(kv[..., :QK_NOPE], kn_nope_g), kv[..., QK_NOPE:]


def attend_block(qn, qr, q_chunk, kn, kr, v, k_chunk):
    s = jnp.einsum('bqhd,bkhd->bhqk', qn, kn) + jnp.einsum('bqhr,bkr->bhqk', qr, kr)
    s = s.astype(jnp.float32) * ATTN_SCALE
    mask = k_chunk[None, :] <= q_chunk[:, None]
    s = jnp.where(mask[None, None], s, NEG)
    p = jax.nn.softmax(s, axis=-1).astype(v.dtype)
    return jnp.einsum('bhqk,bkhd->bqhd', p, v)


def prompt_attention(qn, qr, q_chunk, kn, kr, v, k_chunk):
    b, s = qn.shape[:2]
    nb = s // Q_BLOCK

    def blocks(t):
        return jnp.swapaxes(t.reshape((b, nb, Q_BLOCK) + t.shape[2:]), 0, 1)

    out = lax.map(lambda a: attend_block(a[0], a[1], a[2], kn, kr, v, k_chunk),
                  (blocks(qn), blocks(qr), q_chunk.reshape(nb, Q_BLOCK)))
    return jnp.swapaxes(out, 0, 1).reshape(b, s, MLA_DIM)


def heads(t):
    return t.reshape(t.shape[:-1] + (RWKV_HEADS, RWKV_HEAD))


def rwkv_inputs(p_rw, prev, mu_shift, w0, w2, a0, a2, k_k, k_a):
    shifted = jnp.concatenate([prev[:, None, :].astype(p_rw.dtype), p_rw[:, :-1]], axis=1)
    xm = p_rw + (shifted - p_rw) * mu_shift
    o = RWKV_DIM
    r, k, v = xm[..., :o], xm[..., o:2 * o], xm[..., 2 * o:3 * o]
    wl = xm[..., 3 * o:3 * o + DECAY_LORA]
    al = xm[..., 3 * o + DECAY_LORA:3 * o + DECAY_LORA + AAA_LORA]
    gl = xm[..., 3 * o + DECAY_LORA + AAA_LORA:]
    w_log = -jax.nn.softplus(-(w0 + jnp.tanh(wl) @ w2).astype(jnp.float32)) - 0.5
    decay = jnp.exp(-jnp.exp(w_log))
    a = jax.nn.sigmoid((a0 + al @ a2).astype(jnp.float32))
    kk = heads((k * k_k).astype(jnp.float32))
    kk = kk / jnp.maximum(jnp.sqrt(jnp.sum(kk * kk, axis=-1, keepdims=True)), 1e-12)
    k_eff = k.astype(jnp.float32) * (1.0 + (a - 1.0) * k_a.astype(jnp.float32))
    return (heads(r.astype(jnp.float32)), heads(decay), heads(k_eff),
            heads(v.astype(jnp.float32)), kk, heads(a), gl)


def wkv_scan(s0, r, w, k, v, kk, a):
    def step(S, inp):
        r_t, w_t, k_t, v_t, kk_t, a_t = inp
        sa = jnp.einsum('bhvk,bhk->bhv', S, -kk_t)
        S = (S * w_t[:, :, None, :] + sa[..., None] * (kk_t * a_t)[:, :, None, :]
             + v_t[..., None] * k_t[:, :, None, :])
        return S, jnp.einsum('bhvk,bhk->bhv', S, r_t)

    xs = tuple(jnp.swapaxes(t, 0, 1) for t in (r, w, k, v, kk, a))
    S, ys = lax.scan(step, s0.astype(jnp.float32), xs)
    return S, jnp.swapaxes(ys, 0, 1)


def rwkv_out(y, r, k, v, gl, g2, r_k, lnx_g, lnx_b):
    b, l = y.shape[:2]
    mean = jnp.mean(y, axis=-1, keepdims=True)
    var = jnp.mean(jnp.square(y - mean), axis=-1, keepdims=True)
    yn = ((y - mean) * lax.rsqrt(var + LNX_EPS)).reshape(b, l, RWKV_DIM)
    yn = yn * lnx_g.astype(jnp.float32) + lnx_b.astype(jnp.float32)
    bonus = (jnp.sum(r * k * r_k.astype(jnp.float32), axis=-1, keepdims=True) * v).reshape(b, l, RWKV_DIM)
    g = jax.nn.sigmoid(gl) @ g2
    return (yn + bonus).astype(gl.dtype) * g


def layer_tail(x, attn, rw, attn_out_g, w_out, norm_ffn_g, w_gate, w_up, w_down):
    mix = jnp.concatenate([rmsnorm(attn, attn_out_g), rw], axis=-1)
    h = x + mix @ w_out
    u = rmsnorm(h, norm_ffn_g)
    return h + (jax.nn.silu(u @ w_gate) * (u @ w_up)) @ w_down


def setup_inputs(seed: int = 0) -> dict:
    key = jax.random.key(seed)
    ks = jax.random.split(key, 40)

    def nrm(k, shape, scale):
        return jax.random.normal(k, shape, jnp.float32) * scale

    return {
        'x_prompt': nrm(ks[0], (BATCH, SEQ, D_MODEL), 1.0),
        'x_sample': nrm(ks[1], (DEC_BATCH, DEC_SEQ, D_MODEL), 1.0),
        'cache_kv_latent': nrm(ks[2], (DEC_BATCH, PAST_LEN, KV_LORA), 1.0),
        'cache_k_rope': nrm(ks[3], (DEC_BATCH, PAST_LEN, QK_ROPE), 1.0),
        'state_wkv': nrm(ks[4], (DEC_BATCH, RWKV_HEADS, RWKV_HEAD, RWKV_HEAD), 0.1),
        'state_shift': nrm(ks[5], (DEC_BATCH, RWKV_COLS), 1.0),
        'meta_tokens': nrm(ks[6], (N_META, D_MODEL), 1.0),
        'norm_mix_g': 1.0 + nrm(ks[7], (D_MODEL,), 0.02),
        'w_in': nrm(ks[8], (D_MODEL, IN_COLS), D_MODEL ** -0.5),
        'q_norm_g': 1.0 + nrm(ks[9], (Q_LORA,), 0.02),
        'w_q_up': nrm(ks[10], (Q_LORA, MLA_HEADS * (QK_NOPE + QK_ROPE)), Q_LORA ** -0.5),
        'kv_norm_g': 1.0 + nrm(ks[11], (KV_LORA,), 0.02),
        'w_kv_up': nrm(ks[12], (KV_LORA, MLA_HEADS * (QK_NOPE + V_DIM)), KV_LORA ** -0.5),
        'qn_nope_g': 1.0 + nrm(ks[13], (QK_NOPE,), 0.02),
        'qn_rope_g': 1.0 + nrm(ks[14], (QK_ROPE,), 0.02),
        'kn_nope_g': 1.0 + nrm(ks[15], (QK_NOPE,), 0.02),
        'kn_rope_g': 1.0 + nrm(ks[16], (QK_ROPE,), 0.02),
        'attn_out_g': 1.0 + nrm(ks[17], (MLA_DIM,), 0.02),
        'mu_shift': jax.random.uniform(ks[18], (RWKV_COLS,), jnp.float32, 0.0, 1.0),
        'w0': jax.random.uniform(ks[19], (RWKV_DIM,), jnp.float32, -6.0, -1.0),
        'w2': nrm(ks[20], (DECAY_LORA, RWKV_DIM), 0.1 * DECAY_LORA ** -0.5),
        'a0': nrm(ks[21], (RWKV_DIM,), 0.1),
        'a2': nrm(ks[22], (AAA_LORA, RWKV_DIM), 0.5 * AAA_LORA ** -0.5),
        'g2': nrm(ks[23], (GATE_LORA, RWKV_DIM), GATE_LORA ** -0.5),
        'k_k': 0.85 + nrm(ks[24], (RWKV_DIM,), 0.02),
        'k_a': 1.0 + nrm(ks[25], (RWKV_DIM,), 0.02),
        'r_k': nrm(ks[26], (RWKV_HEADS, RWKV_HEAD), 0.1),
        'lnx_g': 1.0 + nrm(ks[27], (RWKV_DIM,), 0.02),
        'lnx_b': nrm(ks[28], (RWKV_DIM,), 0.02),
        'w_out': nrm(ks[29], (MIX_DIM, D_MODEL), MIX_DIM ** -0.5),
        'norm_ffn_g': 1.0 + nrm(ks[30], (D_MODEL,), 0.02),
        'w_gate': nrm(ks[31], (D_MODEL, D_FF), D_MODEL ** -0.5),
        'w_up': nrm(ks[32], (D_MODEL, D_FF), D_MODEL ** -0.5),
        'w_down': nrm(ks[33], (D_FF, D_MODEL), D_FF ** -0.5),
    }


def reference(x_prompt, x_sample, cache_kv_latent, cache_k_rope, state_wkv, state_shift,
              meta_tokens, norm_mix_g, w_in, q_norm_g, w_q_up, kv_norm_g, w_kv_up,
              qn_nope_g, qn_rope_g, kn_nope_g, kn_rope_g, attn_out_g, mu_shift, w0, w2,
              a0, a2, g2, k_k, k_a, r_k, lnx_g, lnx_b, w_out, norm_ffn_g, w_gate, w_up, w_down):
    b, s = x_prompt.shape[:2]
    L = N_META + s
    xf = jnp.concatenate([jnp.broadcast_to(meta_tokens.astype(x_prompt.dtype)[None], (b, N_META, D_MODEL)),
                          x_prompt], axis=1)
    proj = rmsnorm(xf, norm_mix_g) @ w_in
    p_mla, p_rw = proj[..., :MLA_COLS], proj[..., MLA_COLS:]
    pos = jnp.arange(L, dtype=jnp.int32)
    k_chunk = jnp.concatenate([jnp.full((N_META,), -1, jnp.int32),
                               jnp.arange(s, dtype=jnp.int32) // CHUNK])
    q_chunk = jnp.arange(s, dtype=jnp.int32) // CHUNK
    c_p, kr_p = mla_kv_rows(p_mla, pos, kv_norm_g, kn_rope_g)
    kn_p, v_p = expand_latent(c_p, w_kv_up, kn_nope_g)
    qn_p, qr_p = mla_queries(p_mla[:, N_META:], pos[N_META:], q_norm_g, w_q_up, qn_nope_g, qn_rope_g)
    attn_p = prompt_attention(qn_p, qr_p, q_chunk, kn_p, kr_p, v_p, k_chunk)
    r, w, k, v, kk, a, gl = rwkv_inputs(p_rw, jnp.zeros((b, RWKV_COLS), p_rw.dtype),
                                        mu_shift, w0, w2, a0, a2, k_k, k_a)
    wkv_p, y_rw = wkv_scan(jnp.zeros((b, RWKV_HEADS, RWKV_HEAD, RWKV_HEAD), jnp.float32),
                           r, w, k, v, kk, a)
    rw_p = rwkv_out(y_rw[:, N_META:], r[:, N_META:], k[:, N_META:], v[:, N_META:],
                    gl[:, N_META:], g2, r_k, lnx_g, lnx_b)
    y_prompt = layer_tail(x_prompt, attn_p, rw_p, attn_out_g, w_out, norm_ffn_g, w_gate, w_up, w_down)
    shift_p = p_rw[:, -1]

    db, ds = x_sample.shape[:2]
    past = cache_kv_latent.shape[1]
    proj_s = rmsnorm(x_sample, norm_mix_g) @ w_in
    ps_mla, ps_rw = proj_s[..., :MLA_COLS], proj_s[..., MLA_COLS:]
    pos_s = past + jnp.arange(ds, dtype=jnp.int32)
    c_s, kr_s = mla_kv_rows(ps_mla, pos_s, kv_norm_g, kn_rope_g)
    c_all = jnp.concatenate([cache_kv_latent.astype(c_s.dtype), c_s], axis=1)
    kr_all = jnp.concatenate([cache_k_rope.astype(kr_s.dtype), kr_s], axis=1)
    kn_s, v_s = expand_latent(c_all, w_kv_up, kn_nope_g)
    qn_s, qr_s = mla_queries(ps_mla, pos_s, q_norm_g, w_q_up, qn_nope_g, qn_rope_g)
    k_chunk_s = jnp.arange(past + ds, dtype=jnp.int32) // CHUNK
    attn_s = attend_block(qn_s, qr_s, pos_s // CHUNK, kn_s, kr_all, v_s, k_chunk_s).reshape(db, ds, MLA_DIM)
    r2, w2_, k2, v2, kk2, a2_, gl2 = rwkv_inputs(ps_rw, state_shift, mu_shift, w0, w2, a0, a2, k_k, k_a)
    wkv_s, y_rw_s = wkv_scan(state_wkv, r2, w2_, k2, v2, kk2, a2_)
    rw_s = rwkv_out(y_rw_s, r2, k2, v2, gl2, g2, r_k, lnx_g, lnx_b)
    y_sample = layer_tail(x_sample, attn_s, rw_s, attn_out_g, w_out, norm_ffn_g, w_gate, w_up, w_down)
    shift_s = ps_rw[:, -1].astype(state_shift.dtype)

    return (y_prompt, y_sample, c_p, kr_p, wkv_p.astype(x_prompt.dtype), shift_p,
            c_s, kr_s, wkv_s.astype(state_wkv.dtype), shift_s)
```

```python
import functools

import numpy as np
import jax
import jax.numpy as jnp
from jax import lax
from jax.experimental import pallas as pl
from jax.experimental.pallas import tpu as pltpu

F32 = jnp.float32
BF16 = jnp.bfloat16

D_MODEL = 1024
CHUNK = 64
N_META = 16
EPS = 1e-6
NEG = -1e30
MLA_HEADS = 8
QK_NOPE = 64
QK_ROPE = 32
V_DIM = 64
Q_LORA = 256
KV_LORA = 128
ROPE_BASE = 10000.0
ATTN_SCALE = (QK_NOPE + QK_ROPE) ** -0.5
MLA_DIM = MLA_HEADS * V_DIM
RWKV_HEADS = 8
RWKV_HEAD = 64
RWKV_DIM = RWKV_HEADS * RWKV_HEAD
DECAY_LORA = 64
AAA_LORA = 64
GATE_LORA = 128
LNX_EPS = 64e-5
RWKV_COLS = 3 * RWKV_DIM + DECAY_LORA + AAA_LORA + GATE_LORA
D_FF = -(-8 * D_MODEL // (3 * 256)) * 256

LANE = 128
MXU_N = 256
HEAD_GROUP = LANE
QK_COLS = MLA_HEADS * HEAD_GROUP
MLA_EXT = 512
IN_EXT = MLA_EXT + RWKV_COLS
VMEM_LIMIT = 56 * 1024 * 1024


def _dot(a, b):
    return jnp.dot(a, b, preferred_element_type=F32)


def _dot_nt(a, b):
    return lax.dot_general(a, b, (((1,), (1,)), ((), ())), preferred_element_type=F32)


def _dot_tn(a, b):
    return lax.dot_general(a, b, (((0,), (0,)), ((), ())), preferred_element_type=F32)


def _seg_reduce(x, m):
    hi = x.astype(BF16)
    lo = (x - hi.astype(F32)).astype(BF16)
    outs = []
    for j in range(x.shape[1] // MXU_N):
        sl = slice(j * MXU_N, (j + 1) * MXU_N)
        outs.append(_dot(hi[:, sl], m) + _dot(lo[:, sl], m))
    return outs[0] if len(outs) == 1 else jnp.concatenate(outs, axis=1)


def _full_spec(arr):
    nd = arr.ndim
    return pl.BlockSpec(arr.shape, lambda *_: (0,) * nd)


def _params(sem):
    return pltpu.CompilerParams(dimension_semantics=sem, vmem_limit_bytes=VMEM_LIMIT)


def _inproj_kernel(x_ref, prev0_ref, cosq_ref, sinq_ref, cosk_ref, sink_ref,
                   gmix_ref, win_ref, gql_ref, wqa_ref, wqb_ref, gqa_ref, gqb_ref, mq_ref,
                   gkv_ref, gkr_ref, gkrs_ref, wkk_ref, gkn_ref, mk_ref, erep_ref, wv_ref,
                   mu_ref, w0_ref, w2_ref, a0_ref, a2_ref, kkg_ref, ka_ref, m64_ref,
                   c_ref, kr_ref, q_ref, kc_ref, vc_ref,
                   r_ref, lw_ref, k_ref, v_ref, kk_ref, b_ref, gl_ref, shift_ref,
                   carry_ref):
    i = pl.program_id(1)
    ts = x_ref.shape[1]

    x = x_ref[0]
    xn = x * lax.rsqrt(jnp.mean(x * x, axis=-1, keepdims=True) + EPS) * gmix_ref[...]
    p = _dot(xn.astype(BF16), win_ref[...])

    pq = p[:, :Q_LORA]
    ql = pq * lax.rsqrt(jnp.mean(pq * pq, axis=-1, keepdims=True) + EPS) * gql_ref[...]
    ql = ql.astype(BF16)
    qa = _dot(ql, wqa_ref[...])
    qb = _dot(ql, wqb_ref[...])
    qs = lax.rsqrt(_seg_reduce(qa * qa, mq_ref[...]) + EPS)
    cosq = jnp.concatenate([cosq_ref[...]] * MLA_HEADS, axis=1)
    sinq = jnp.concatenate([sinq_ref[...]] * MLA_HEADS, axis=1)
    q = qs * (qa * gqa_ref[...] * cosq + qb * gqb_ref[...] * sinq) * ATTN_SCALE
    q_ref[0] = q.astype(BF16)

    pc = p[:, Q_LORA:Q_LORA + KV_LORA]
    c = pc * lax.rsqrt(jnp.mean(pc * pc, axis=-1, keepdims=True) + EPS) * gkv_ref[...]
    c_ref[0] = c
    cb = c.astype(BF16)

    pk = p[:, Q_LORA + KV_LORA:MLA_EXT]
    lane = lax.broadcasted_iota(jnp.int32, pk.shape, 1)
    ssk = jnp.sum(jnp.where(lane < QK_ROPE, pk * pk, 0.0), axis=-1, keepdims=True)
    sk = lax.rsqrt(ssk * (1.0 / QK_ROPE) + EPS)
    pk_sw = pltpu.roll(pk, LANE - QK_ROPE, axis=1)
    kr = sk * (pk * gkr_ref[...] * cosk_ref[...] + pk_sw * gkrs_ref[...] * sink_ref[...])
    kr_ref[0] = kr[:, :QK_ROPE]

    kraw = _dot(cb, wkk_ref[...])
    kn = kraw * lax.rsqrt(_seg_reduce(kraw * kraw, mk_ref[...]) + EPS) * gkn_ref[...]
    kc_ref[0] = (kn + _dot(kr.astype(BF16), erep_ref[...])).astype(BF16)
    vc_ref[0] = _dot(cb, wv_ref[...]).astype(BF16)

    prw = p[:, MLA_EXT:]

    @pl.when(i == 0)
    def _():
        carry_ref[...] = prev0_ref[0]

    row = lax.broadcasted_iota(jnp.int32, (ts, 1), 0)
    shifted = jnp.where(row == 0, carry_ref[...], pltpu.roll(prw, 1, axis=0))
    last = prw[ts - 1:ts, :]
    carry_ref[...] = last
    shift_ref[0] = last
    xm = prw + (shifted - prw) * mu_ref[...]

    o = RWKV_DIM
    r = xm[:, :o]
    k = xm[:, o:2 * o]
    v = xm[:, 2 * o:3 * o]
    wa = xm[:, 3 * o:3 * o + DECAY_LORA + AAA_LORA]
    gl_ref[0] = xm[:, 3 * o + DECAY_LORA + AAA_LORA:]
    dw = _dot(jnp.tanh(wa).astype(BF16), w2_ref[...])
    da = _dot(wa.astype(BF16), a2_ref[...])
    lw_ref[0] = -jax.nn.sigmoid(w0_ref[...] + dw) * float(np.exp(-0.5))
    a = jax.nn.sigmoid(a0_ref[...] + da)
    kx = k * kkg_ref[...]
    kk = kx * lax.rsqrt(jnp.maximum(_seg_reduce(kx * kx, m64_ref[...]), 1e-24))
    r_ref[0] = r
    k_ref[0] = k * (1.0 + (a - 1.0) * ka_ref[...])
    v_ref[0] = v
    kk_ref[0] = kk
    b_ref[0] = kk * a


def _inproj(x, prev0, tabs, wts, ts):
    b, t, _ = x.shape
    nb = prev0.shape[0]
    grid = (b, t // ts)
    tok = lambda w: pl.BlockSpec((1, ts, w), lambda bi, i: (bi, i, 0))
    in_specs = [tok(D_MODEL),
                pl.BlockSpec((1, 1, RWKV_COLS), (lambda bi, i: (bi, 0, 0)) if nb > 1 else (lambda bi, i: (0, 0, 0)))]
    in_specs += [pl.BlockSpec((ts, LANE), lambda bi, i: (i, 0)) for _ in tabs]
    in_specs += [_full_spec(w) for w in wts]
    widths = [(KV_LORA, F32), (QK_ROPE, F32), (QK_COLS, BF16), (QK_COLS, BF16), (MLA_DIM, BF16)]
    widths += [(RWKV_DIM, F32)] * 6 + [(GATE_LORA, F32)]
    out_shape = [jax.ShapeDtypeStruct((b, t, w), dt) for w, dt in widths]
    out_specs = [tok(w) for w, _ in widths]
    out_shape.append(jax.ShapeDtypeStruct((b, 1, RWKV_COLS), F32))
    out_specs.append(pl.BlockSpec((1, 1, RWKV_COLS), lambda bi, i: (bi, 0, 0)))
    return pl.pallas_call(
        _inproj_kernel, grid=grid, in_specs=in_specs, out_specs=out_specs, out_shape=out_shape,
        scratch_shapes=[pltpu.VMEM((1, RWKV_COLS), F32)],
        compiler_params=_params(("arbitrary", "arbitrary")), name="inproj",
    )(x, prev0, *tabs, *wts)


def _wkv_chunk(c, csz, refs, consts, states):
    r_ref, lw_ref, k_ref, v_ref, kk_ref, b_ref, gl_ref = refs
    tri, g2, rk, lng, lnb, m64mean, m64sum = consts
    sl = pl.ds(pl.multiple_of(c * csz, csz), csz)
    r = r_ref[0, sl, :]
    lw = lw_ref[0, sl, :]
    k = k_ref[0, sl, :]
    v = v_ref[0, sl, :]
    kk = kk_ref[0, sl, :]
    b = b_ref[0, sl, :]
    gl = gl_ref[0, sl, :]

    cum = jnp.dot(tri, lw, preferred_element_type=F32, precision=lax.Precision.HIGHEST)
    g = jnp.exp(cum)
    gi = jnp.exp(-cum)
    gp = jnp.exp(cum - lw)
    gc = g[csz - 1:csz, :]
    at = -(kk * gp)
    bt = b * gi
    kt = k * gi
    rt = r * g
    btc = bt * gc
    ktc = kt * gc

    rowi = lax.broadcasted_iota(jnp.int32, (2 * csz, 2 * csz), 0)
    coli = lax.broadcasted_iota(jnp.int32, (2 * csz, 2 * csz), 1)
    colt = jnp.where(coli >= csz, coli - csz, coli)
    gmask = colt < jnp.where(rowi < csz, rowi, rowi - csz + 1)
    eye = (lax.broadcasted_iota(jnp.int32, (csz, csz), 0)
           == lax.broadcasted_iota(jnp.int32, (csz, csz), 1)).astype(F32)

    ys = []
    new_states = []
    for h in range(RWKV_HEADS):
        hs = slice(h * RWKV_HEAD, (h + 1) * RWKV_HEAD)
        s0 = states[h]
        ar = jnp.concatenate([at[:, hs], rt[:, hs]], axis=0).astype(BF16)
        bk = jnp.concatenate([bt[:, hs], kt[:, hs]], axis=0).astype(BF16)
        gm = jnp.where(gmask, _dot_nt(ar, bk), 0.0)
        a_s = _dot_nt(ar, s0.astype(BF16))
        vc = v[:, hs]
        n = gm[:csz, :csz]
        rhs = a_s[:csz] + _dot(gm[:csz, csz:].astype(BF16), vc.astype(BF16))
        tm = eye + n
        nb16 = n.astype(BF16)
        pw = _dot(nb16, nb16)
        steps = int(np.log2(csz)) - 2
        for _ in range(steps):
            xx = _dot(pw.astype(BF16), jnp.concatenate([pw, tm], axis=1).astype(BF16))
            pw = xx[:, :csz]
            tm = tm + xx[:, csz:]
        w = _dot(tm.astype(BF16), rhs.astype(BF16))
        u = w + _dot(pw.astype(BF16), w.astype(BF16))
        uv = jnp.concatenate([u, vc], axis=0).astype(BF16)
        ys.append(a_s[csz:] + _dot(gm[csz:, :].astype(BF16), uv))
        bkc = jnp.concatenate([btc[:, hs], ktc[:, hs]], axis=0).astype(BF16)
        new_states.append(s0 * gc[:, hs] + _dot_tn(uv, bkc))

    y = jnp.concatenate(ys, axis=1)
    mean = _seg_reduce(y, m64mean)
    d = y - mean
    var = _seg_reduce(d * d, m64mean)
    yn = d * lax.rsqrt(var + LNX_EPS) * lng + lnb
    bonus = _seg_reduce(r * k * rk, m64sum) * v
    gate = _dot(jax.nn.sigmoid(gl).astype(BF16), g2)
    return new_states, (yn + bonus) * gate


def _wkv_kernel(r_ref, lw_ref, k_ref, v_ref, kk_ref, b_ref, gl_ref, s0_ref,
                tri_ref, g2_ref, rk_ref, lng_ref, lnb_ref, m64mean_ref, m64sum_ref,
                rw_ref, sout_ref, state_ref, *, csz):
    i = pl.program_id(1)
    tt = r_ref.shape[1]

    @pl.when(i == 0)
    def _():
        state_ref[...] = s0_ref[0]

    refs = (r_ref, lw_ref, k_ref, v_ref, kk_ref, b_ref, gl_ref)
    consts = (tri_ref[...], g2_ref[...], rk_ref[...], lng_ref[...], lnb_ref[...],
              m64mean_ref[...], m64sum_ref[...])

    def body(c, carry):
        states = [state_ref[h] for h in range(RWKV_HEADS)]
        new_states, rw = _wkv_chunk(c, csz, refs, consts, states)
        for h in range(RWKV_HEADS):
            state_ref[h] = new_states[h]
        rw_ref[0, pl.ds(pl.multiple_of(c * csz, csz), csz), :] = rw.astype(rw_ref.dtype)
        return carry

    lax.fori_loop(0, tt // csz, body, 0)
    sout_ref[0] = state_ref[...]


def _wkv(seqs, s0, consts, csz, tt):
    b, t, _ = seqs[0].shape
    nb = s0.shape[0]
    tok = lambda w: pl.BlockSpec((1, tt, w), lambda bi, i: (bi, i, 0))
    st_shape = (1, RWKV_HEADS, RWKV_HEAD, RWKV_HEAD)
    in_specs = [tok(RWKV_DIM)] * 6 + [tok(GATE_LORA)]
    in_specs.append(pl.BlockSpec(st_shape, (lambda bi, i: (bi, 0, 0, 0)) if nb > 1 else (lambda bi, i: (0, 0, 0, 0))))
    tri = jnp.tril(jnp.ones((csz, csz), F32))
    consts = (tri,) + tuple(consts)
    in_specs += [_full_spec(w) for w in consts]
    return pl.pallas_call(
        functools.partial(_wkv_kernel, csz=csz), grid=(b, t // tt),
        in_specs=in_specs,
        out_specs=[tok(RWKV_DIM), pl.BlockSpec(st_shape, lambda bi, i: (bi, 0, 0, 0))],
        out_shape=[jax.ShapeDtypeStruct((b, t, RWKV_DIM), BF16),
                   jax.ShapeDtypeStruct((b,) + st_shape[1:], F32)],
        scratch_shapes=[pltpu.VMEM(st_shape[1:], F32)],
        compiler_params=_params(("arbitrary", "arbitrary")), name="wkv",
    )(*seqs, s0, *consts)


def _softmax_step(qh, kt, vt, mask, m, l, acc):
    s = _dot_nt(qh, kt)
    if mask is not None:
        s = jnp.where(mask, s, NEG)
    m_new = jnp.maximum(m, jnp.max(s, axis=-1, keepdims=True))
    alpha = jnp.exp(m - m_new)
    pr = jnp.exp(s - m_new)
    l_new = alpha * l + jnp.sum(pr, axis=-1, keepdims=True)
    acc_new = alpha * acc + _dot(pr.astype(BF16), vt)
    return m_new, l_new, acc_new


def _finish_heads(accs, ls, g):
    tq = accs[0].shape[0]
    lane = lax.broadcasted_iota(jnp.int32, (tq, LANE), 1)
    pairs = []
    for pr in range(MLA_HEADS // 2):
        oa = accs[2 * pr] / ls[2 * pr]
        ob = accs[2 * pr + 1] / ls[2 * pr + 1]
        pairs.append(jnp.where(lane < V_DIM, oa, ob))
    o = jnp.concatenate(pairs, axis=1)
    return o * lax.rsqrt(jnp.mean(o * o, axis=-1, keepdims=True) + EPS) * g


def _attn_prompt_kernel(q_ref, k_ref, v_ref, km_ref, vm_ref, g_ref, o_ref, *, tk):
    i = pl.program_id(1)
    tq = q_ref.shape[1]
    assert tq == tk
    meta_mask = lax.broadcasted_iota(jnp.int32, (tq, LANE), 1) < N_META
    qrow = lax.broadcasted_iota(jnp.int32, (tq, tk), 0) // CHUNK
    kcol = lax.broadcasted_iota(jnp.int32, (tq, tk), 1) // CHUNK
    diag_mask = kcol <= qrow
    accs, ls = [], []
    for h in range(MLA_HEADS):
        hq = slice(h * HEAD_GROUP, (h + 1) * HEAD_GROUP)
        hv = slice((h // 2) * LANE, (h // 2 + 1) * LANE)
        qh = q_ref[0, :, hq]
        m = jnp.full((tq, 1), NEG, F32)
        l = jnp.zeros((tq, 1), F32)
        acc = jnp.zeros((tq, LANE), F32)
        m, l, acc = _softmax_step(qh, km_ref[:, hq], vm_ref[:, hv], meta_mask, m, l, acc)

        def body(kt, carry, qh=qh, hq=hq, hv=hv):
            ks = pl.ds(pl.multiple_of(kt * tk, tk), tk)
            return _softmax_step(qh, k_ref[0, ks, hq], v_ref[0, ks, hv], None, *carry)

        m, l, acc = lax.fori_loop(0, i, body, (m, l, acc))
        ks = pl.ds(pl.multiple_of(i * tk, tk), tk)
        m, l, acc = _softmax_step(qh, k_ref[0, ks, hq], v_ref[0, ks, hv], diag_mask, m, l, acc)
        accs.append(acc)
        ls.append(l)
    o_ref[0] = _finish_heads(accs, ls, g_ref[...]).astype(o_ref.dtype)


def _attn_prompt(q, kc, vc, kmeta, vmeta, g, tq):
    b, s, _ = q.shape
    return pl.pallas_call(
        functools.partial(_attn_prompt_kernel, tk=tq), grid=(b, s // tq),
        in_specs=[pl.BlockSpec((1, tq, QK_COLS), lambda bi, i: (bi, i, 0)),
                  pl.BlockSpec((1, s, QK_COLS), lambda bi, i: (bi, 0, 0)),
                  pl.BlockSpec((1, s, MLA_DIM), lambda bi, i: (bi, 0, 0)),
                  _full_spec(kmeta), _full_spec(vmeta), _full_spec(g)],
        out_specs=pl.BlockSpec((1, tq, MLA_DIM), lambda bi, i: (bi, i, 0)),
        out_shape=jax.ShapeDtypeStruct((b, s, MLA_DIM), BF16),
        compiler_params=_params(("arbitrary", "arbitrary")), name="attn_prompt",
    )(q, kc, vc, kmeta, vmeta, g)


def _attn_sample_kernel(q_ref, kn_ref, vn_ref, cc_ref, ck_ref, wkk_ref, gkn_ref, mk_ref,
                        erep_ref, wv_ref, g_ref, o_ref, *, tk, past):
    tq = q_ref.shape[1]
    ntiles = past // tk
    qs = [q_ref[0, :, h * HEAD_GROUP:(h + 1) * HEAD_GROUP] for h in range(MLA_HEADS)]
    qchunk = (past + lax.broadcasted_iota(jnp.int32, (tq, 1), 0)) // CHUNK

    def heads_step(kc, vc, mask, carry):
        ms, ls, accs = carry
        nm, nl, na = [], [], []
        for h in range(MLA_HEADS):
            hq = slice(h * HEAD_GROUP, (h + 1) * HEAD_GROUP)
            hv = slice((h // 2) * LANE, (h // 2 + 1) * LANE)
            m, l, a = _softmax_step(qs[h], kc[:, hq], vc[:, hv], mask, ms[h], ls[h], accs[h])
            nm.append(m)
            nl.append(l)
            na.append(a)
        return tuple(nm), tuple(nl), tuple(na)

    def body(t, carry):
        ks = pl.ds(pl.multiple_of(t * tk, tk), tk)
        cb = cc_ref[0, ks, :].astype(BF16)
        kraw = _dot(cb, wkk_ref[...])
        kn = kraw * lax.rsqrt(_seg_reduce(kraw * kraw, mk_ref[...]) + EPS) * gkn_ref[...]
        kc = (kn + _dot(ck_ref[0, ks, :].astype(BF16), erep_ref[...])).astype(BF16)
        vc = _dot(cb, wv_ref[...]).astype(BF16)
        kchunk = (t * tk + lax.broadcasted_iota(jnp.int32, (1, tk), 1)) // CHUNK
        return heads_step(kc, vc, kchunk <= qchunk, carry)

    init = (tuple(jnp.full((tq, 1), NEG, F32) for _ in range(MLA_HEADS)),
            tuple(jnp.zeros((tq, 1), F32) for _ in range(MLA_HEADS)),
            tuple(jnp.zeros((tq, LANE), F32) for _ in range(MLA_HEADS)))
    carry = lax.fori_loop(0, ntiles, body, init)
    kchunk = (past + lax.broadcasted_iota(jnp.int32, (1, tq), 1)) // CHUNK
    ms, ls, accs = heads_step(kn_ref[0], vn_ref[0], kchunk <= qchunk, carry)
    o_ref[0] = _finish_heads(accs, ls, g_ref[...]).astype(o_ref.dtype)


def _attn_sample(q, kn, vn, cache_c, cache_kr, wts, g, tk):
    b, t, _ = q.shape
    past = cache_c.shape[1]
    bspec = lambda shp: pl.BlockSpec((1,) + shp, lambda bi: (bi, 0, 0))
    return pl.pallas_call(
        functools.partial(_attn_sample_kernel, tk=tk, past=past), grid=(b,),
        in_specs=[bspec((t, QK_COLS)), bspec((t, QK_COLS)), bspec((t, MLA_DIM)),
                  bspec((past, KV_LORA)), bspec((past, QK_ROPE))]
                 + [_full_spec(w) for w in wts] + [_full_spec(g)],
        out_specs=bspec((t, MLA_DIM)),
        out_shape=jax.ShapeDtypeStruct((b, t, MLA_DIM), BF16),
        compiler_params=_params(("arbitrary",)), name="attn_sample",
    )(q, kn, vn, cache_c, cache_kr, *wts, g)


def _tail_kernel(x_ref, at_ref, rw_ref, wo_ref, gf_ref, wg_ref, wu_ref, wd_ref, o_ref, *, fc):
    mix = jnp.concatenate([at_ref[...], rw_ref[...]], axis=1)
    h = x_ref[...] + _dot(mix, wo_ref[...])
    u = (h * lax.rsqrt(jnp.mean(h * h, axis=-1, keepdims=True) + EPS) * gf_ref[...]).astype(BF16)
    acc = h
    for j in range(D_FF // fc):
        cs = slice(j * fc, (j + 1) * fc)
        gt = _dot(u, wg_ref[:, cs])
        up = _dot(u, wu_ref[:, cs])
        act = (gt * jax.nn.sigmoid(gt) * up).astype(BF16)
        acc = acc + _dot(act, wd_ref[cs, :])
    o_ref[...] = acc


def _tail(x, attn, rw, wts, ts):
    n = x.shape[0]
    tok = lambda w: pl.BlockSpec((ts, w), lambda i: (i, 0))
    wspec = lambda w: pl.BlockSpec(w.shape, lambda i: (0,) * w.ndim, pipeline_mode=pl.Buffered(1))
    return pl.pallas_call(
        functools.partial(_tail_kernel, fc=MXU_N), grid=(n // ts,),
        in_specs=[tok(D_MODEL), tok(MLA_DIM), tok(RWKV_DIM)] + [wspec(w) for w in wts],
        out_specs=tok(D_MODEL),
        out_shape=jax.ShapeDtypeStruct((n, D_MODEL), F32),
        compiler_params=_params(("arbitrary",)), name="tail",
    )(x, attn, rw, *wts)


def _block_matrix(size, seg_of, scale_of):
    seg = np.array([seg_of(i) for i in range(size)])
    m = np.zeros((size, size), np.float32)
    for i in range(size):
        if seg[i] >= 0:
            m[i, seg == seg[i]] = scale_of(i)
    return jnp.asarray(m, BF16)


def _rope_tables(pos):
    half = QK_ROPE // 2
    inv = ROPE_BASE ** (-jnp.arange(half, dtype=F32) / half)
    ang = pos.astype(F32)[:, None] * inv[None, :]
    cos, sin = jnp.cos(ang), jnp.sin(ang)
    t = pos.shape[0]
    cc = jnp.concatenate([cos, cos], axis=1)
    ss = jnp.concatenate([-sin, sin], axis=1)
    zq = jnp.zeros((t, LANE - QK_NOPE - QK_ROPE), F32)
    cosq = jnp.concatenate([jnp.ones((t, QK_NOPE), F32), cc, zq], axis=1)
    sinq = jnp.concatenate([jnp.zeros((t, QK_NOPE), F32), ss, zq], axis=1)
    zk = jnp.zeros((t, LANE - QK_ROPE), F32)
    return cosq, sinq, jnp.concatenate([cc, zk], axis=1), jnp.concatenate([ss, zk], axis=1)


def _prep_weights(norm_mix_g, w_in, q_norm_g, w_q_up, kv_norm_g, w_kv_up, qn_nope_g, qn_rope_g,
                  kn_nope_g, kn_rope_g, mu_shift, w0, w2, a0, a2, k_k, k_a):
    half = QK_ROPE // 2
    swap = np.concatenate([np.arange(half, QK_ROPE), np.arange(half)])
    row = lambda v: v.astype(F32).reshape(1, -1)
    pad_to = lambda v, n: jnp.concatenate([v, jnp.zeros(v.shape[:-1] + (n - v.shape[-1],), v.dtype)], axis=-1)

    mla_cols = Q_LORA + KV_LORA + QK_ROPE
    w_kr = w_in[:, Q_LORA + KV_LORA:mla_cols]
    win = jnp.concatenate([pad_to(jnp.concatenate([w_in[:, :mla_cols], w_kr[:, swap]], axis=1), MLA_EXT),
                           w_in[:, mla_cols:]], axis=1).astype(BF16)

    qh = w_q_up.reshape(Q_LORA, MLA_HEADS, QK_NOPE + QK_ROPE)
    zq = jnp.zeros((Q_LORA, MLA_HEADS, QK_NOPE), w_q_up.dtype)
    wqa = pad_to(qh, HEAD_GROUP).reshape(Q_LORA, QK_COLS).astype(BF16)
    wqb = pad_to(jnp.concatenate([zq, qh[:, :, QK_NOPE:][:, :, swap]], axis=2), HEAD_GROUP)
    wqb = wqb.reshape(Q_LORA, QK_COLS).astype(BF16)
    gqa = jnp.tile(pad_to(jnp.concatenate([qn_nope_g, qn_rope_g]), HEAD_GROUP), MLA_HEADS)
    gqb = jnp.tile(pad_to(jnp.concatenate([jnp.zeros_like(qn_nope_g), qn_rope_g[swap]]), HEAD_GROUP), MLA_HEADS)

    def seg_q(i):
        j = i % HEAD_GROUP
        base = (i // HEAD_GROUP) * 2
        return base if j < QK_NOPE else (base + 1 if j < QK_NOPE + QK_ROPE else -1)
    mq = _block_matrix(MXU_N, seg_q, lambda i: 1.0 / (QK_NOPE if i % HEAD_GROUP < QK_NOPE else QK_ROPE))
    mk = _block_matrix(MXU_N, lambda i: i // HEAD_GROUP if i % HEAD_GROUP < QK_NOPE else -1,
                       lambda i: 1.0 / QK_NOPE)
    m64sum = _block_matrix(MXU_N, lambda i: i // RWKV_HEAD, lambda i: 1.0)
    m64mean = _block_matrix(MXU_N, lambda i: i // RWKV_HEAD, lambda i: 1.0 / RWKV_HEAD)

    kvh = w_kv_up.reshape(KV_LORA, MLA_HEADS, QK_NOPE + V_DIM)
    wkk = pad_to(kvh[:, :, :QK_NOPE], HEAD_GROUP).reshape(KV_LORA, QK_COLS).astype(BF16)
    wv = kvh[:, :, QK_NOPE:].reshape(KV_LORA, MLA_DIM).astype(BF16)
    gkn = jnp.tile(pad_to(kn_nope_g, HEAD_GROUP), MLA_HEADS)
    erep_np = np.zeros((LANE, QK_COLS), np.float32)
    for h in range(MLA_HEADS):
        for j in range(QK_ROPE):
            erep_np[j, h * HEAD_GROUP + QK_NOPE + j] = 1.0
    erep = jnp.asarray(erep_np, BF16)
    gkr = pad_to(kn_rope_g, LANE)
    gkrs = pad_to(kn_rope_g[swap], LANE)

    w2e = jnp.concatenate([w2, jnp.zeros((AAA_LORA, RWKV_DIM), w2.dtype)], axis=0).astype(BF16)
    a2e = jnp.concatenate([jnp.zeros((DECAY_LORA, RWKV_DIM), a2.dtype), a2], axis=0).astype(BF16)

    inproj_w = [row(norm_mix_g), win, row(q_norm_g), wqa, wqb, row(gqa), row(gqb), mq,
                row(kv_norm_g), row(gkr), row(gkrs), wkk, row(gkn), mk, erep, wv,
                row(mu_shift), row(w0), w2e, row(a0), a2e, row(k_k), row(k_a), m64sum]
    sample_w = [wkk, row(gkn), mk, erep[:QK_ROPE], wv]
    return inproj_w, sample_w, m64mean, m64sum


def kernel(x_prompt, x_sample, cache_kv_latent, cache_k_rope, state_wkv, state_shift, meta_tokens, norm_mix_g, w_in, q_norm_g, w_q_up, kv_norm_g, w_kv_up, qn_nope_g, qn_rope_g, kn_nope_g, kn_rope_g, attn_out_g, mu_shift, w0, w2, a0, a2, g2, k_k, k_a, r_k, lnx_g, lnx_b, w_out, norm_ffn_g, w_gate, w_up, w_down):
    b, s, _ = x_prompt.shape
    db, ds, _ = x_sample.shape
    past = cache_kv_latent.shape[1]
    row = lambda v: v.astype(F32).reshape(1, -1)

    inproj_w, sample_w, m64mean, m64sum = _prep_weights(
        norm_mix_g, w_in, q_norm_g, w_q_up, kv_norm_g, w_kv_up, qn_nope_g, qn_rope_g,
        kn_nope_g, kn_rope_g, mu_shift, w0, w2, a0, a2, k_k, k_a)
    wkv_consts = (g2.astype(BF16), row(r_k), row(lnx_g), row(lnx_b), m64mean, m64sum)
    tail_w = [w_out.astype(BF16), row(norm_ffn_g), w_gate.astype(BF16), w_up.astype(BF16), w_down.astype(BF16)]
    g_attn = row(attn_out_g)

    ts = min(512, s)
    meta = _inproj(meta_tokens.astype(F32)[None], jnp.zeros((1, 1, RWKV_COLS), F32),
                   _rope_tables(jnp.arange(N_META)), inproj_w, N_META)
    c_m, kr_m, _, kc_m, vc_m = meta[:5]
    _, s_meta = _wkv(meta[5:12], jnp.zeros((1, RWKV_HEADS, RWKV_HEAD, RWKV_HEAD), F32),
                     wkv_consts, N_META, N_META)

    pr = _inproj(x_prompt, meta[12], _rope_tables(N_META + jnp.arange(s)), inproj_w, ts)
    c_p, kr_p, q_p, kc_p, vc_p = pr[:5]
    rw_p, wkv_p = _wkv(pr[5:12], s_meta, wkv_consts, CHUNK, min(256, s))
    pad_meta = lambda a: jnp.pad(a[0], ((0, LANE - N_META), (0, 0)))
    attn_p = _attn_prompt(q_p, kc_p, vc_p, pad_meta(kc_m), pad_meta(vc_m), g_attn, min(256, s))
    y_prompt = _tail(x_prompt.reshape(b * s, D_MODEL), attn_p.reshape(b * s, MLA_DIM),
                     rw_p.reshape(b * s, RWKV_DIM), tail_w, ts).reshape(b, s, D_MODEL)
    kv_latent_p = jnp.concatenate([jnp.broadcast_to(c_m, (b, N_META, KV_LORA)), c_p], axis=1)
    k_rope_p = jnp.concatenate([jnp.broadcast_to(kr_m, (b, N_META, QK_ROPE)), kr_p], axis=1)

    sm = _inproj(x_sample, state_shift.astype(F32)[:, None, :], _rope_tables(past + jnp.arange(ds)),
                 inproj_w, ds)
    c_s, kr_s, q_s, kc_s, vc_s = sm[:5]
    rw_s, wkv_s = _wkv(sm[5:12], state_wkv.astype(F32), wkv_consts, ds, ds)
    attn_s = _attn_sample(q_s, kc_s, vc_s, cache_kv_latent, cache_k_rope, sample_w, g_attn, min(512, past))
    y_sample = _tail(x_sample.reshape(db * ds, D_MODEL), attn_s.reshape(db * ds, MLA_DIM),
                     rw_s.reshape(db * ds, RWKV_DIM), tail_w, min(512, db * ds)).reshape(db, ds, D_MODEL)

    return (y_prompt, y_sample, kv_latent_p, k_rope_p, wkv_p, pr[12][:, 0, :],
            c_s, kr_s, wkv_s, sm[12][:, 0, :])
```

```python
import functools

import numpy as np
import jax
import jax.numpy as jnp
from jax import lax
from jax.experimental import pallas as pl
from jax.experimental.pallas import tpu as pltpu

F32 = jnp.float32
BF16 = jnp.bfloat16

D_MODEL = 1024
CHUNK = 64
N_META = 16
EPS = 1e-6
NEG = -1e30
MLA_HEADS = 8
QK_NOPE = 64
QK_ROPE = 32
V_DIM = 64
Q_LORA = 256
KV_LORA = 128
ROPE_BASE = 10000.0
ATTN_SCALE = (QK_NOPE + QK_ROPE) ** -0.5
MLA_DIM = MLA_HEADS * V_DIM
RWKV_HEADS = 8
RWKV_HEAD = 64
RWKV_DIM = RWKV_HEADS * RWKV_HEAD
DECAY_LORA = 64
AAA_LORA = 64
GATE_LORA = 128
LNX_EPS = 64e-5
RWKV_COLS = 3 * RWKV_DIM + DECAY_LORA + AAA_LORA + GATE_LORA
D_FF = -(-8 * D_MODEL // (3 * 256)) * 256

LANE = 128
MXU_N = 256
HEAD_GROUP = LANE
QK_COLS = MLA_HEADS * HEAD_GROUP
MLA_EXT = 512
IN_EXT = MLA_EXT + RWKV_COLS
VMEM_LIMIT = 56 * 1024 * 1024


def _dot(a, b):
    return jnp.dot(a, b, preferred_element_type=F32)


def _dot_nt(a, b):
    return lax.dot_general(a, b, (((1,), (1,)), ((), ())), preferred_element_type=F32)


def _dot_tn(a, b):
    return lax.dot_general(a, b, (((0,), (0,)), ((), ())), preferred_element_type=F32)


def _seg_reduce(x, m):
    hi = x.astype(BF16)
    lo = (x - hi.astype(F32)).astype(BF16)
    outs = []
    for j in range(x.shape[1] // MXU_N):
        sl = slice(j * MXU_N, (j + 1) * MXU_N)
        outs.append(_dot(hi[:, sl], m) + _dot(lo[:, sl], m))
    return outs[0] if len(outs) == 1 else jnp.concatenate(outs, axis=1)


def _full_spec(arr):
    nd = arr.ndim
    return pl.BlockSpec(arr.shape, lambda *_: (0,) * nd)


def _params(sem):
    return pltpu.CompilerParams(dimension_semantics=sem, vmem_limit_bytes=VMEM_LIMIT)


def _inproj_kernel(x_ref, prev0_ref, cosq_ref, sinq_ref, cosk_ref, sink_ref,
                   gmix_ref, win_ref, gql_ref, wqa_ref, wqb_ref, gqa_ref, gqb_ref, mq_ref,
                   gkv_ref, gkr_ref, gkrs_ref, wkk_ref, gkn_ref, mk_ref, erep_ref, wv_ref,
                   mu_ref, w0_ref, w2_ref, a0_ref, a2_ref, kkg_ref, ka_ref, m64_ref,
                   c_ref, kr_ref, q_ref, kc_ref, vc_ref,
                   r_ref, lw_ref, k_ref, v_ref, kk_ref, b_ref, gl_ref, shift_ref,
                   carry_ref):
    i = pl.program_id(1)
    ts = x_ref.shape[1]

    x = x_ref[0]
    xn = x * lax.rsqrt(jnp.mean(x * x, axis=-1, keepdims=True) + EPS) * gmix_ref[...]
    p = _dot(xn.astype(BF16), win_ref[...])

    pq = p[:, :Q_LORA]
    ql = pq * lax.rsqrt(jnp.mean(pq * pq, axis=-1, keepdims=True) + EPS) * gql_ref[...]
    ql = ql.astype(BF16)
    qa = _dot(ql, wqa_ref[...])
    qb = _dot(ql, wqb_ref[...])
    qs = lax.rsqrt(_seg_reduce(qa * qa, mq_ref[...]) + EPS)
    cosq = jnp.concatenate([cosq_ref[...]] * MLA_HEADS, axis=1)
    sinq = jnp.concatenate([sinq_ref[...]] * MLA_HEADS, axis=1)
    q = qs * (qa * gqa_ref[...] * cosq + qb * gqb_ref[...] * sinq) * ATTN_SCALE
    q_ref[0] = q.astype(BF16)

    pc = p[:, Q_LORA:Q_LORA + KV_LORA]
    c = pc * lax.rsqrt(jnp.mean(pc * pc, axis=-1, keepdims=True) + EPS) * gkv_ref[...]
    c_ref[0] = c
    cb = c.astype(BF16)

    pk = p[:, Q_LORA + KV_LORA:MLA_EXT]
    lane = lax.broadcasted_iota(jnp.int32, pk.shape, 1)
    ssk = jnp.sum(jnp.where(lane < QK_ROPE, pk * pk, 0.0), axis=-1, keepdims=True)
    sk = lax.rsqrt(ssk * (1.0 / QK_ROPE) + EPS)
    pk_sw = pltpu.roll(pk, LANE - QK_ROPE, axis=1)
    kr = sk * (pk * gkr_ref[...] * cosk_ref[...] + pk_sw * gkrs_ref[...] * sink_ref[...])
    kr_ref[0] = kr[:, :QK_ROPE]

    kraw = _dot(cb, wkk_ref[...])
    kn = kraw * lax.rsqrt(_seg_reduce(kraw * kraw, mk_ref[...]) + EPS) * gkn_ref[...]
    kc_ref[0] = (kn + _dot(kr.astype(BF16), erep_ref[...])).astype(BF16)
    vc_ref[0] = _dot(cb, wv_ref[...]).astype(BF16)

    prw = p[:, MLA_EXT:]

    @pl.when(i == 0)
    def _():
        carry_ref[...] = prev0_ref[0]

    row = lax.broadcasted_iota(jnp.int32, (ts, 1), 0)
    shifted = jnp.where(row == 0, carry_ref[...], pltpu.roll(prw, 1, axis=0))
    last = prw[ts - 1:ts, :]
    carry_ref[...] = last
    shift_ref[0] = last
    xm = prw + (shifted - prw) * mu_ref[...]

    o = RWKV_DIM
    r = xm[:, :o]
    k = xm[:, o:2 * o]
    v = xm[:, 2 * o:3 * o]
    wa = xm[:, 3 * o:3 * o + DECAY_LORA + AAA_LORA]
    gl_ref[0] = xm[:, 3 * o + DECAY_LORA + AAA_LORA:]
    dw = _dot(jnp.tanh(wa).astype(BF16), w2_ref[...])
    da = _dot(wa.astype(BF16), a2_ref[...])
    lw_ref[0] = -jax.nn.sigmoid(w0_ref[...] + dw) * float(np.exp(-0.5))
    a = jax.nn.sigmoid(a0_ref[...] + da)
    kx = k * kkg_ref[...]
    kk = kx * lax.rsqrt(jnp.maximum(_seg_reduce(kx * kx, m64_ref[...]), 1e-24))
    r_ref[0] = r
    k_ref[0] = k * (1.0 + (a - 1.0) * ka_ref[...])
    v_ref[0] = v
    kk_ref[0] = kk
    b_ref[0] = kk * a


def _inproj(x, prev0, tabs, wts, ts):
    b, t, _ = x.shape
    nb = prev0.shape[0]
    grid = (b, t // ts)
    tok = lambda w: pl.BlockSpec((1, ts, w), lambda bi, i: (bi, i, 0))
    in_specs = [tok(D_MODEL),
                pl.BlockSpec((1, 1, RWKV_COLS), (lambda bi, i: (bi, 0, 0)) if nb > 1 else (lambda bi, i: (0, 0, 0)))]
    in_specs += [pl.BlockSpec((ts, LANE), lambda bi, i: (i, 0)) for _ in tabs]
    in_specs += [_full_spec(w) for w in wts]
    widths = [(KV_LORA, F32), (QK_ROPE, F32), (QK_COLS, BF16), (QK_COLS, BF16), (MLA_DIM, BF16)]
    widths += [(RWKV_DIM, F32)] * 6 + [(GATE_LORA, F32)]
    out_shape = [jax.ShapeDtypeStruct((b, t, w), dt) for w, dt in widths]
    out_specs = [tok(w) for w, _ in widths]
    out_shape.append(jax.ShapeDtypeStruct((b, 1, RWKV_COLS), F32))
    out_specs.append(pl.BlockSpec((1, 1, RWKV_COLS), lambda bi, i: (bi, 0, 0)))
    return pl.pallas_call(
        _inproj_kernel, grid=grid, in_specs=in_specs, out_specs=out_specs, out_shape=out_shape,
        scratch_shapes=[pltpu.VMEM((1, RWKV_COLS), F32)],
        compiler_params=_params(("arbitrary", "arbitrary")), name="inproj",
    )(x, prev0, *tabs, *wts)


PAIRS = RWKV_HEADS // 2
PAIR_W = 2 * RWKV_HEAD


def _wkv_kernel(r_ref, lw_ref, k_ref, v_ref, kk_ref, b_ref, gl_ref, s0_ref,
                tri_ref, g2_ref, rk_ref, lng_ref, lnb_ref, m64mean_ref, m64sum_ref,
                rw_ref, sout_ref, state_ref):
    i = pl.program_id(1)
    tt = r_ref.shape[1]
    c = CHUNK
    nch = tt // c

    @pl.when(i == 0)
    def _():
        state_ref[...] = s0_ref[0]

    r = r_ref[0]
    lw = lw_ref[0]
    k = k_ref[0]
    v = v_ref[0]
    cum = jnp.dot(tri_ref[...], lw, preferred_element_type=F32, precision=lax.Precision.HIGHEST)
    g = jnp.exp(cum)
    gi = jnp.exp(-cum)
    gp = jnp.exp(cum - lw)
    gc_rows = [g[j * c + c - 1:(j + 1) * c, :] for j in range(nch)]
    gcb = jnp.concatenate([jnp.broadcast_to(x, (c, RWKV_DIM)) for x in gc_rows], axis=0)
    at = -(kk_ref[0] * gp)
    bt = b_ref[0] * gi
    kt = k * gi
    rt = r * g

    even = (lax.broadcasted_iota(jnp.int32, (tt, RWKV_DIM), 1) % PAIR_W) < RWKV_HEAD
    split = lambda x: (jnp.where(even, x, 0.0).astype(BF16), jnp.where(even, 0.0, x).astype(BF16))
    atm_b, rtm_b, vm_b = split(at), split(rt), split(v)
    bt_b, kt_b, v_b = bt.astype(BF16), kt.astype(BF16), v.astype(BF16)
    btc_b, ktc_b = (bt * gcb).astype(BF16), (kt * gcb).astype(BF16)

    lane = lax.broadcasted_iota(jnp.int32, (c, PAIR_W), 1)
    lo = lane < c
    eye_hi = (lane == lax.broadcasted_iota(jnp.int32, (c, PAIR_W), 0) + c).astype(F32)
    rowi = lax.broadcasted_iota(jnp.int32, (2 * c, 2 * c), 0)
    coli = lax.broadcasted_iota(jnp.int32, (2 * c, 2 * c), 1)
    colt = jnp.where(coli >= c, coli - c, coli)
    gmask = colt < jnp.where(rowi < c, rowi, rowi - c + 1)
    blk = (rowi // c) == (coli // c)
    z1 = jnp.zeros((c, PAIR_W), BF16)
    z2 = jnp.zeros((c, 2 * PAIR_W), BF16)

    items = [(j, h) for j in range(nch) for h in range(RWKV_HEADS)]
    rows = lambda j: slice(j * c, (j + 1) * c)
    lanes = lambda h: slice((h // 2) * PAIR_W, (h // 2 + 1) * PAIR_W)
    cut = lambda x, j, h: x[rows(j), lanes(h)]
    vstack = lambda a, b: jnp.concatenate([a, b], axis=0)
    hstack = lambda a, b: jnp.concatenate([a, b], axis=1)

    gb, xs = [], []
    for j, h in items:
        ar = vstack(cut(atm_b[h % 2], j, h), cut(rtm_b[h % 2], j, h))
        bk = vstack(cut(bt_b, j, h), cut(kt_b, j, h))
        gm = jnp.where(gmask, _dot_nt(ar, bk), 0.0)
        gb.append(gm.astype(BF16))
        xs.append(jnp.where(lo, gm[:c], eye_hi))
    for _ in range(int(np.log2(c)) - 1):
        xs = [_dot(x[:, :c].astype(BF16), x.astype(BF16)) + jnp.where(lo, 0.0, x) for x in xs]
    xb = [x.astype(BF16) for x in xs]
    vm = [cut(vm_b[h % 2], j, h) for j, h in items]
    akv = [_dot(gb[n][:c], vstack(z1, vm[n])) for n in range(len(items))]
    zc = [hstack(cut(atm_b[h % 2], j, h), akv[n].astype(BF16)) for n, (j, h) in enumerate(items)]
    w = [_dot(xb[n], vstack(z2, zc[n])) for n in range(len(items))]
    tu = [w[n] + _dot(xb[n], vstack(w[n].astype(BF16), z2)) for n in range(len(items))]
    ry = [_dot(gb[n][c:], vstack(tu[n].astype(BF16), hstack(z1, vm[n]))) for n in range(len(items))]

    reff, yv, pm, qm = {}, {}, {}, {}
    for j in range(nch):
        for p in range(PAIRS):
            n0 = j * RWKV_HEADS + 2 * p
            tu_p = tu[n0] + tu[n0 + 1]
            ry_p = ry[n0] + ry[n0 + 1]
            h = 2 * p
            reff[j, p] = (cut(rt, j, h) + ry_p[:, :PAIR_W]).astype(BF16)
            yv[j, p] = ry_p[:, PAIR_W:]
            pm[j, p] = jnp.where(blk, _dot_tn(tu_p[:, :PAIR_W].astype(BF16), cut(btc_b, j, h)), 0.0).astype(BF16)
            qm[j, p] = jnp.where(blk, _dot_tn(vstack(tu_p[:, PAIR_W:].astype(BF16), cut(v_b, j, h)),
                                              vstack(cut(btc_b, j, h), cut(ktc_b, j, h))), 0.0)

    st = [state_ref[p] for p in range(PAIRS)]
    ys = []
    for j in range(nch):
        sb = [s.astype(BF16) for s in st]
        ys.append(jnp.concatenate([_dot_nt(reff[j, p], sb[p]) + yv[j, p] for p in range(PAIRS)], axis=1))
        st = [st[p] * gc_rows[j][:, p * PAIR_W:(p + 1) * PAIR_W] + _dot(sb[p], pm[j, p]) + qm[j, p]
              for p in range(PAIRS)]
    for p in range(PAIRS):
        state_ref[p] = st[p]
    sout_ref[0] = state_ref[...]

    y = ys[0] if nch == 1 else jnp.concatenate(ys, axis=0)
    mean = _seg_reduce(y, m64mean_ref[...])
    d = y - mean
    var = _seg_reduce(d * d, m64mean_ref[...])
    yn = d * lax.rsqrt(var + LNX_EPS) * lng_ref[...] + lnb_ref[...]
    bonus = _seg_reduce(r * k * rk_ref[...], m64sum_ref[...]) * v
    gate = _dot(jax.nn.sigmoid(gl_ref[0]).astype(BF16), g2_ref[...])
    rw_ref[0] = ((yn + bonus) * gate).astype(rw_ref.dtype)


def _state_to_pairs(s):
    z = jnp.zeros_like(s[:, 0::2])
    top = jnp.concatenate([s[:, 0::2], z], axis=-1)
    bot = jnp.concatenate([z, s[:, 1::2]], axis=-1)
    return jnp.concatenate([top, bot], axis=2)


def _pairs_to_state(sp):
    b = sp.shape[0]
    s = jnp.stack([sp[:, :, :RWKV_HEAD, :RWKV_HEAD], sp[:, :, RWKV_HEAD:, RWKV_HEAD:]], axis=2)
    return s.reshape(b, RWKV_HEADS, RWKV_HEAD, RWKV_HEAD)


def _wkv(seqs, s0, consts, tt):
    b, t, _ = seqs[0].shape
    nb = s0.shape[0]
    tok = lambda w: pl.BlockSpec((1, tt, w), lambda bi, i: (bi, i, 0))
    st_shape = (1, PAIRS, PAIR_W, PAIR_W)
    in_specs = [tok(RWKV_DIM)] * 6 + [tok(GATE_LORA)]
    in_specs.append(pl.BlockSpec(st_shape, (lambda bi, i: (bi, 0, 0, 0)) if nb > 1 else (lambda bi, i: (0, 0, 0, 0))))
    tri = jnp.kron(jnp.eye(tt // CHUNK, dtype=F32), jnp.tril(jnp.ones((CHUNK, CHUNK), F32)))
    consts = (tri,) + tuple(consts)
    in_specs += [_full_spec(w) for w in consts]
    rw, sp = pl.pallas_call(
        _wkv_kernel, grid=(b, t // tt),
        in_specs=in_specs,
        out_specs=[tok(RWKV_DIM), pl.BlockSpec(st_shape, lambda bi, i: (bi, 0, 0, 0))],
        out_shape=[jax.ShapeDtypeStruct((b, t, RWKV_DIM), BF16),
                   jax.ShapeDtypeStruct((b,) + st_shape[1:], F32)],
        scratch_shapes=[pltpu.VMEM(st_shape[1:], F32)],
        compiler_params=_params(("arbitrary", "arbitrary")), name="wkv",
    )(*seqs, _state_to_pairs(s0), *consts)
    return rw, _pairs_to_state(sp)


def _pad_chunk(seqs):
    t = seqs[0].shape[1]
    pad = (-t) % CHUNK
    return [jnp.pad(a, ((0, 0), (0, pad), (0, 0))) for a in seqs]


def _softmax_step(qh, kt, vt, mask, m, l, acc):
    s = _dot_nt(qh, kt)
    if mask is not None:
        s = jnp.where(mask, s, NEG)
    m_new = jnp.maximum(m, jnp.max(s, axis=-1, keepdims=True))
    alpha = jnp.exp(m - m_new)
    pr = jnp.exp(s - m_new)
    l_new = alpha * l + jnp.sum(pr, axis=-1, keepdims=True)
    acc_new = alpha * acc + _dot(pr.astype(BF16), vt)
    return m_new, l_new, acc_new


def _finish_heads(accs, ls, g):
    tq = accs[0].shape[0]
    lane = lax.broadcasted_iota(jnp.int32, (tq, LANE), 1)
    pairs = []
    for pr in range(MLA_HEADS // 2):
        oa = accs[2 * pr] / ls[2 * pr]
        ob = accs[2 * pr + 1] / ls[2 * pr + 1]
        pairs.append(jnp.where(lane < V_DIM, oa, ob))
    o = jnp.concatenate(pairs, axis=1)
    return o * lax.rsqrt(jnp.mean(o * o, axis=-1, keepdims=True) + EPS) * g


def _attn_prompt_kernel(q_ref, k_ref, v_ref, km_ref, vm_ref, g_ref, o_ref, *, tk):
    i = pl.program_id(1)
    tq = q_ref.shape[1]
    assert tq == tk
    meta_mask = lax.broadcasted_iota(jnp.int32, (tq, LANE), 1) < N_META
    qrow = lax.broadcasted_iota(jnp.int32, (tq, tk), 0) // CHUNK
    kcol = lax.broadcasted_iota(jnp.int32, (tq, tk), 1) // CHUNK
    diag_mask = kcol <= qrow
    accs, ls = [], []
    for h in range(MLA_HEADS):
        hq = slice(h * HEAD_GROUP, (h + 1) * HEAD_GROUP)
        hv = slice((h // 2) * LANE, (h // 2 + 1) * LANE)
        qh = q_ref[0, :, hq]
        m = jnp.full((tq, 1), NEG, F32)
        l = jnp.zeros((tq, 1), F32)
        acc = jnp.zeros((tq, LANE), F32)
        m, l, acc = _softmax_step(qh, km_ref[:, hq], vm_ref[:, hv], meta_mask, m, l, acc)

        def body(kt, carry, qh=qh, hq=hq, hv=hv):
            ks = pl.ds(pl.multiple_of(kt * tk, tk), tk)
            return _softmax_step(qh, k_ref[0, ks, hq], v_ref[0, ks, hv], None, *carry)

        m, l, acc = lax.fori_loop(0, i, body, (m, l, acc))
        ks = pl.ds(pl.multiple_of(i * tk, tk), tk)
        m, l, acc = _softmax_step(qh, k_ref[0, ks, hq], v_ref[0, ks, hv], diag_mask, m, l, acc)
        accs.append(acc)
        ls.append(l)
    o_ref[0] = _finish_heads(accs, ls, g_ref[...]).astype(o_ref.dtype)


def _attn_prompt(q, kc, vc, kmeta, vmeta, g, tq):
    b, s, _ = q.shape
    return pl.pallas_call(
        functools.partial(_attn_prompt_kernel, tk=tq), grid=(b, s // tq),
        in_specs=[pl.BlockSpec((1, tq, QK_COLS), lambda bi, i: (bi, i, 0)),
                  pl.BlockSpec((1, s, QK_COLS), lambda bi, i: (bi, 0, 0)),
                  pl.BlockSpec((1, s, MLA_DIM), lambda bi, i: (bi, 0, 0)),
                  _full_spec(kmeta), _full_spec(vmeta), _full_spec(g)],
        out_specs=pl.BlockSpec((1, tq, MLA_DIM), lambda bi, i: (bi, i, 0)),
        out_shape=jax.ShapeDtypeStruct((b, s, MLA_DIM), BF16),
        compiler_params=_params(("arbitrary", "arbitrary")), name="attn_prompt",
    )(q, kc, vc, kmeta, vmeta, g)


def _attn_sample_kernel(q_ref, kn_ref, vn_ref, cc_ref, ck_ref, wkk_ref, gkn_ref, mk_ref,
                        erep_ref, wv_ref, g_ref, o_ref, *, tk, past):
    tq = q_ref.shape[1]
    ntiles = past // tk
    qs = [q_ref[0, :, h * HEAD_GROUP:(h + 1) * HEAD_GROUP] for h in range(MLA_HEADS)]
    qchunk = (past + lax.broadcasted_iota(jnp.int32, (tq, 1), 0)) // CHUNK

    def heads_step(kc, vc, mask, carry):
        ms, ls, accs = carry
        nm, nl, na = [], [], []
        for h in range(MLA_HEADS):
            hq = slice(h * HEAD_GROUP, (h + 1) * HEAD_GROUP)
            hv = slice((h // 2) * LANE, (h // 2 + 1) * LANE)
            m, l, a = _softmax_step(qs[h], kc[:, hq], vc[:, hv], mask, ms[h], ls[h], accs[h])
            nm.append(m)
            nl.append(l)
            na.append(a)
        return tuple(nm), tuple(nl), tuple(na)

    def body(t, carry):
        ks = pl.ds(pl.multiple_of(t * tk, tk), tk)
        cb = cc_ref[0, ks, :].astype(BF16)
        kraw = _dot(cb, wkk_ref[...])
        kn = kraw * lax.rsqrt(_seg_reduce(kraw * kraw, mk_ref[...]) + EPS) * gkn_ref[...]
        kc = (kn + _dot(ck_ref[0, ks, :].astype(BF16), erep_ref[...])).astype(BF16)
        vc = _dot(cb, wv_ref[...]).astype(BF16)
        kchunk = (t * tk + lax.broadcasted_iota(jnp.int32, (1, tk), 1)) // CHUNK
        return heads_step(kc, vc, kchunk <= qchunk, carry)

    init = (tuple(jnp.full((tq, 1), NEG, F32) for _ in range(MLA_HEADS)),
            tuple(jnp.zeros((tq, 1), F32) for _ in range(MLA_HEADS)),
            tuple(jnp.zeros((tq, LANE), F32) for _ in range(MLA_HEADS)))
    carry = lax.fori_loop(0, ntiles, body, init)
    kchunk = (past + lax.broadcasted_iota(jnp.int32, (1, tq), 1)) // CHUNK
    ms, ls, accs = heads_step(kn_ref[0], vn_ref[0], kchunk <= qchunk, carry)
    o_ref[0] = _finish_heads(accs, ls, g_ref[...]).astype(o_ref.dtype)


def _attn_sample(q, kn, vn, cache_c, cache_kr, wts, g, tk):
    b, t, _ = q.shape
    past = cache_c.shape[1]
    bspec = lambda shp: pl.BlockSpec((1,) + shp, lambda bi: (bi, 0, 0))
    return pl.pallas_call(
        functools.partial(_attn_sample_kernel, tk=tk, past=past), grid=(b,),
        in_specs=[bspec((t, QK_COLS)), bspec((t, QK_COLS)), bspec((t, MLA_DIM)),
                  bspec((past, KV_LORA)), bspec((past, QK_ROPE))]
                 + [_full_spec(w) for w in wts] + [_full_spec(g)],
        out_specs=bspec((t, MLA_DIM)),
        out_shape=jax.ShapeDtypeStruct((b, t, MLA_DIM), BF16),
        compiler_params=_params(("arbitrary",)), name="attn_sample",
    )(q, kn, vn, cache_c, cache_kr, *wts, g)


def _tail_kernel(x_ref, at_ref, rw_ref, wo_ref, gf_ref, wg_ref, wu_ref, wd_ref, o_ref, *, fc):
    mix = jnp.concatenate([at_ref[...], rw_ref[...]], axis=1)
    h = x_ref[...] + _dot(mix, wo_ref[...])
    u = (h * lax.rsqrt(jnp.mean(h * h, axis=-1, keepdims=True) + EPS) * gf_ref[...]).astype(BF16)
    acc = h
    for j in range(D_FF // fc):
        cs = slice(j * fc, (j + 1) * fc)
        gt = _dot(u, wg_ref[:, cs])
        up = _dot(u, wu_ref[:, cs])
        act = (gt * jax.nn.sigmoid(gt) * up).astype(BF16)
        acc = acc + _dot(act, wd_ref[cs, :])
    o_ref[...] = acc


def _tail(x, attn, rw, wts, ts):
    n = x.shape[0]
    tok = lambda w: pl.BlockSpec((ts, w), lambda i: (i, 0))
    wspec = lambda w: pl.BlockSpec(w.shape, lambda i: (0,) * w.ndim, pipeline_mode=pl.Buffered(1))
    return pl.pallas_call(
        functools.partial(_tail_kernel, fc=MXU_N), grid=(n // ts,),
        in_specs=[tok(D_MODEL), tok(MLA_DIM), tok(RWKV_DIM)] + [wspec(w) for w in wts],
        out_specs=tok(D_MODEL),
        out_shape=jax.ShapeDtypeStruct((n, D_MODEL), F32),
        compiler_params=_params(("arbitrary",)), name="tail",
    )(x, attn, rw, *wts)


def _block_matrix(size, seg_of, scale_of):
    seg = np.array([seg_of(i) for i in range(size)])
    m = np.zeros((size, size), np.float32)
    for i in range(size):
        if seg[i] >= 0:
            m[i, seg == seg[i]] = scale_of(i)
    return jnp.asarray(m, BF16)


def _rope_tables(pos):
    half = QK_ROPE // 2
    inv = ROPE_BASE ** (-jnp.arange(half, dtype=F32) / half)
    ang = pos.astype(F32)[:, None] * inv[None, :]
    cos, sin = jnp.cos(ang), jnp.sin(ang)
    t = pos.shape[0]
    cc = jnp.concatenate([cos, cos], axis=1)
    ss = jnp.concatenate([-sin, sin], axis=1)
    zq = jnp.zeros((t, LANE - QK_NOPE - QK_ROPE), F32)
    cosq = jnp.concatenate([jnp.ones((t, QK_NOPE), F32), cc, zq], axis=1)
    sinq = jnp.concatenate([jnp.zeros((t, QK_NOPE), F32), ss, zq], axis=1)
    zk = jnp.zeros((t, LANE - QK_ROPE), F32)
    return cosq, sinq, jnp.concatenate([cc, zk], axis=1), jnp.concatenate([ss, zk], axis=1)


def _prep_weights(norm_mix_g, w_in, q_norm_g, w_q_up, kv_norm_g, w_kv_up, qn_nope_g, qn_rope_g,
                  kn_nope_g, kn_rope_g, mu_shift, w0, w2, a0, a2, k_k, k_a):
    half = QK_ROPE // 2
    swap = np.concatenate([np.arange(half, QK_ROPE), np.arange(half)])
    row = lambda v: v.astype(F32).reshape(1, -1)
    pad_to = lambda v, n: jnp.concatenate([v, jnp.zeros(v.shape[:-1] + (n - v.shape[-1],), v.dtype)], axis=-1)

    mla_cols = Q_LORA + KV_LORA + QK_ROPE
    w_kr = w_in[:, Q_LORA + KV_LORA:mla_cols]
    win = jnp.concatenate([pad_to(jnp.concatenate([w_in[:, :mla_cols], w_kr[:, swap]], axis=1), MLA_EXT),
                           w_in[:, mla_cols:]], axis=1).astype(BF16)

    qh = w_q_up.reshape(Q_LORA, MLA_HEADS, QK_NOPE + QK_ROPE)
    zq = jnp.zeros((Q_LORA, MLA_HEADS, QK_NOPE), w_q_up.dtype)
    wqa = pad_to(qh, HEAD_GROUP).reshape(Q_LORA, QK_COLS).astype(BF16)
    wqb = pad_to(jnp.concatenate([zq, qh[:, :, QK_NOPE:][:, :, swap]], axis=2), HEAD_GROUP)
    wqb = wqb.reshape(Q_LORA, QK_COLS).astype(BF16)
    gqa = jnp.tile(pad_to(jnp.concatenate([qn_nope_g, qn_rope_g]), HEAD_GROUP), MLA_HEADS)
    gqb = jnp.tile(pad_to(jnp.concatenate([jnp.zeros_like(qn_nope_g), qn_rope_g[swap]]), HEAD_GROUP), MLA_HEADS)

    def seg_q(i):
        j = i % HEAD_GROUP
        base = (i // HEAD_GROUP) * 2
        return base if j < QK_NOPE else (base + 1 if j < QK_NOPE + QK_ROPE else -1)
    mq = _block_matrix(MXU_N, seg_q, lambda i: 1.0 / (QK_NOPE if i % HEAD_GROUP < QK_NOPE else QK_ROPE))
    mk = _block_matrix(MXU_N, lambda i: i // HEAD_GROUP if i % HEAD_GROUP < QK_NOPE else -1,
                       lambda i: 1.0 / QK_NOPE)
    m64sum = _block_matrix(MXU_N, lambda i: i // RWKV_HEAD, lambda i: 1.0)
    m64mean = _block_matrix(MXU_N, lambda i: i // RWKV_HEAD, lambda i: 1.0 / RWKV_HEAD)

    kvh = w_kv_up.reshape(KV_LORA, MLA_HEADS, QK_NOPE + V_DIM)
    wkk = pad_to(kvh[:, :, :QK_NOPE], HEAD_GROUP).reshape(KV_LORA, QK_COLS).astype(BF16)
    wv = kvh[:, :, QK_NOPE:].reshape(KV_LORA, MLA_DIM).astype(BF16)
    gkn = jnp.tile(pad_to(kn_nope_g, HEAD_GROUP), MLA_HEADS)
    erep_np = np.zeros((LANE, QK_COLS), np.float32)
    for h in range(MLA_HEADS):
        for j in range(QK_ROPE):
            erep_np[j, h * HEAD_GROUP + QK_NOPE + j] = 1.0
    erep = jnp.asarray(erep_np, BF16)
    gkr = pad_to(kn_rope_g, LANE)
    gkrs = pad_to(kn_rope_g[swap], LANE)

    w2e = jnp.concatenate([w2, jnp.zeros((AAA_LORA, RWKV_DIM), w2.dtype)], axis=0).astype(BF16)
    a2e = jnp.concatenate([jnp.zeros((DECAY_LORA, RWKV_DIM), a2.dtype), a2], axis=0).astype(BF16)

    inproj_w = [row(norm_mix_g), win, row(q_norm_g), wqa, wqb, row(gqa), row(gqb), mq,
                row(kv_norm_g), row(gkr), row(gkrs), wkk, row(gkn), mk, erep, wv,
                row(mu_shift), row(w0), w2e, row(a0), a2e, row(k_k), row(k_a), m64sum]
    sample_w = [wkk, row(gkn), mk, erep[:QK_ROPE], wv]
    return inproj_w, sample_w, m64mean, m64sum


def kernel(x_prompt, x_sample, cache_kv_latent, cache_k_rope, state_wkv, state_shift, meta_tokens, norm_mix_g, w_in, q_norm_g, w_q_up, kv_norm_g, w_kv_up, qn_nope_g, qn_rope_g, kn_nope_g, kn_rope_g, attn_out_g, mu_shift, w0, w2, a0, a2, g2, k_k, k_a, r_k, lnx_g, lnx_b, w_out, norm_ffn_g, w_gate, w_up, w_down):
    b, s, _ = x_prompt.shape
    db, ds, _ = x_sample.shape
    past = cache_kv_latent.shape[1]
    row = lambda v: v.astype(F32).reshape(1, -1)

    inproj_w, sample_w, m64mean, m64sum = _prep_weights(
        norm_mix_g, w_in, q_norm_g, w_q_up, kv_norm_g, w_kv_up, qn_nope_g, qn_rope_g,
        kn_nope_g, kn_rope_g, mu_shift, w0, w2, a0, a2, k_k, k_a)
    wkv_consts = (g2.astype(BF16), row(r_k), row(lnx_g), row(lnx_b), m64mean, m64sum)
    tail_w = [w_out.astype(BF16), row(norm_ffn_g), w_gate.astype(BF16), w_up.astype(BF16), w_down.astype(BF16)]
    g_attn = row(attn_out_g)

    ts = min(512, s)
    meta = _inproj(meta_tokens.astype(F32)[None], jnp.zeros((1, 1, RWKV_COLS), F32),
                   _rope_tables(jnp.arange(N_META)), inproj_w, N_META)
    c_m, kr_m, _, kc_m, vc_m = meta[:5]
    _, s_meta = _wkv(_pad_chunk(meta[5:12]), jnp.zeros((1, RWKV_HEADS, RWKV_HEAD, RWKV_HEAD), F32),
                     wkv_consts, CHUNK)

    pr = _inproj(x_prompt, meta[12], _rope_tables(N_META + jnp.arange(s)), inproj_w, ts)
    c_p, kr_p, q_p, kc_p, vc_p = pr[:5]
    rw_p, wkv_p = _wkv(pr[5:12], s_meta, wkv_consts, 2 * CHUNK)
    pad_meta = lambda a: jnp.pad(a[0], ((0, LANE - N_META), (0, 0)))
    attn_p = _attn_prompt(q_p, kc_p, vc_p, pad_meta(kc_m), pad_meta(vc_m), g_attn, min(256, s))
    y_prompt = _tail(x_prompt.reshape(b * s, D_MODEL), attn_p.reshape(b * s, MLA_DIM),
                     rw_p.reshape(b * s, RWKV_DIM), tail_w, ts).reshape(b, s, D_MODEL)
    kv_latent_p = jnp.concatenate([jnp.broadcast_to(c_m, (b, N_META, KV_LORA)), c_p], axis=1)
    k_rope_p = jnp.concatenate([jnp.broadcast_to(kr_m, (b, N_META, QK_ROPE)), kr_p], axis=1)

    sm = _inproj(x_sample, state_shift.astype(F32)[:, None, :], _rope_tables(past + jnp.arange(ds)),
                 inproj_w, ds)
    c_s, kr_s, q_s, kc_s, vc_s = sm[:5]
    rw_s, wkv_s = _wkv(_pad_chunk(sm[5:12]), state_wkv.astype(F32), wkv_consts, CHUNK)
    rw_s = rw_s[:, :ds]
    attn_s = _attn_sample(q_s, kc_s, vc_s, cache_kv_latent, cache_k_rope, sample_w, g_attn, min(512, past))
    y_sample = _tail(x_sample.reshape(db * ds, D_MODEL), attn_s.reshape(db * ds, MLA_DIM),
                     rw_s.reshape(db * ds, RWKV_DIM), tail_w, min(512, db * ds)).reshape(db, ds, D_MODEL)

    return (y_prompt, y_sample, kv_latent_p, k_rope_p, wkv_p, pr[12][:, 0, :],
            c_s, kr_s, wkv_s, sm[12][:, 0, :])
```

```python
import functools

import numpy as np
import jax
import jax.numpy as jnp
from jax import lax
from jax.experimental import pallas as pl
from jax.experimental.pallas import tpu as pltpu

F32 = jnp.float32
BF16 = jnp.bfloat16

D_MODEL = 1024
CHUNK = 64
N_META = 16
EPS = 1e-6
NEG = -1e30
MLA_HEADS = 8
QK_NOPE = 64
QK_ROPE = 32
V_DIM = 64
Q_LORA = 256
KV_LORA = 128
ROPE_BASE = 10000.0
ATTN_SCALE = (QK_NOPE + QK_ROPE) ** -0.5
MLA_DIM = MLA_HEADS * V_DIM
RWKV_HEADS = 8
RWKV_HEAD = 64
RWKV_DIM = RWKV_HEADS * RWKV_HEAD
DECAY_LORA = 64
AAA_LORA = 64
GATE_LORA = 128
LNX_EPS = 64e-5
RWKV_COLS = 3 * RWKV_DIM + DECAY_LORA + AAA_LORA + GATE_LORA
D_FF = -(-8 * D_MODEL // (3 * 256)) * 256

LANE = 128
MXU_N = 256
HEAD_GROUP = LANE
QK_COLS = MLA_HEADS * HEAD_GROUP
MLA_EXT = 512
IN_EXT = MLA_EXT + RWKV_COLS
VMEM_LIMIT = 56 * 1024 * 1024


def _dot(a, b):
    return jnp.dot(a, b, preferred_element_type=F32)


def _dot_nt(a, b):
    return lax.dot_general(a, b, (((1,), (1,)), ((), ())), preferred_element_type=F32)


def _dot_tn(a, b):
    return lax.dot_general(a, b, (((0,), (0,)), ((), ())), preferred_element_type=F32)


def _seg_reduce(x, m):
    hi = x.astype(BF16)
    lo = (x - hi.astype(F32)).astype(BF16)
    outs = []
    for j in range(x.shape[1] // MXU_N):
        sl = slice(j * MXU_N, (j + 1) * MXU_N)
        outs.append(_dot(hi[:, sl], m) + _dot(lo[:, sl], m))
    return outs[0] if len(outs) == 1 else jnp.concatenate(outs, axis=1)


def _full_spec(arr):
    nd = arr.ndim
    return pl.BlockSpec(arr.shape, lambda *_: (0,) * nd)


def _params(sem):
    return pltpu.CompilerParams(dimension_semantics=sem, vmem_limit_bytes=VMEM_LIMIT)


def _inproj_kernel(x_ref, prev0_ref, cosq_ref, sinq_ref, cosk_ref, sink_ref,
                   gmix_ref, win_ref, gql_ref, wqa_ref, wqb_ref, gqa_ref, gqb_ref, mq_ref,
                   gkv_ref, gkr_ref, gkrs_ref, wkk_ref, gkn_ref, mk_ref, erep_ref, wv_ref, wvt_ref,
                   mu_ref, w0_ref, w2_ref, a0_ref, a2_ref, kkg_ref, ka_ref, m64_ref,
                   c_ref, kr_ref, q_ref, kc_ref, vc_ref,
                   r_ref, lw_ref, k_ref, v_ref, kk_ref, b_ref, gl_ref, shift_ref,
                   carry_ref, *, values_transposed):
    i = pl.program_id(1)
    ts = x_ref.shape[1]

    x = x_ref[0]
    xn = x * lax.rsqrt(jnp.mean(x * x, axis=-1, keepdims=True) + EPS) * gmix_ref[...]
    p = _dot(xn.astype(BF16), win_ref[...])

    pq = p[:, :Q_LORA]
    ql = pq * lax.rsqrt(jnp.mean(pq * pq, axis=-1, keepdims=True) + EPS) * gql_ref[...]
    ql = ql.astype(BF16)
    qa = _dot(ql, wqa_ref[...])
    qb = _dot(ql, wqb_ref[...])
    qs = lax.rsqrt(_seg_reduce(qa * qa, mq_ref[...]) + EPS)
    cosq = jnp.concatenate([cosq_ref[...]] * MLA_HEADS, axis=1)
    sinq = jnp.concatenate([sinq_ref[...]] * MLA_HEADS, axis=1)
    q = qs * (qa * gqa_ref[...] * cosq + qb * gqb_ref[...] * sinq) * ATTN_SCALE
    q_ref[0] = q.astype(BF16)

    pc = p[:, Q_LORA:Q_LORA + KV_LORA]
    c = pc * lax.rsqrt(jnp.mean(pc * pc, axis=-1, keepdims=True) + EPS) * gkv_ref[...]
    c_ref[0] = c
    cb = c.astype(BF16)

    pk = p[:, Q_LORA + KV_LORA:MLA_EXT]
    lane = lax.broadcasted_iota(jnp.int32, pk.shape, 1)
    ssk = jnp.sum(jnp.where(lane < QK_ROPE, pk * pk, 0.0), axis=-1, keepdims=True)
    sk = lax.rsqrt(ssk * (1.0 / QK_ROPE) + EPS)
    pk_sw = pltpu.roll(pk, LANE - QK_ROPE, axis=1)
    kr = sk * (pk * gkr_ref[...] * cosk_ref[...] + pk_sw * gkrs_ref[...] * sink_ref[...])
    kr_ref[0] = kr[:, :QK_ROPE]

    kraw = _dot(cb, wkk_ref[...])
    kn = kraw * lax.rsqrt(_seg_reduce(kraw * kraw, mk_ref[...]) + EPS) * gkn_ref[...]
    kc_ref[0] = (kn + _dot(kr.astype(BF16), erep_ref[...])).astype(BF16)
    if values_transposed:
        vc_ref[0] = _dot_nt(wvt_ref[...], cb).astype(BF16)
    else:
        vc_ref[0] = _dot(cb, wv_ref[...]).astype(BF16)

    prw = p[:, MLA_EXT:]

    @pl.when(i == 0)
    def _():
        carry_ref[...] = prev0_ref[0]

    row = lax.broadcasted_iota(jnp.int32, (ts, 1), 0)
    shifted = jnp.where(row == 0, carry_ref[...], pltpu.roll(prw, 1, axis=0))
    last = prw[ts - 1:ts, :]
    carry_ref[...] = last
    shift_ref[0] = last
    xm = prw + (shifted - prw) * mu_ref[...]

    o = RWKV_DIM
    r = xm[:, :o]
    k = xm[:, o:2 * o]
    v = xm[:, 2 * o:3 * o]
    wa = xm[:, 3 * o:3 * o + DECAY_LORA + AAA_LORA]
    gl_ref[0] = xm[:, 3 * o + DECAY_LORA + AAA_LORA:]
    dw = _dot(jnp.tanh(wa).astype(BF16), w2_ref[...])
    da = _dot(wa.astype(BF16), a2_ref[...])
    lw_ref[0] = -jax.nn.sigmoid(w0_ref[...] + dw) * float(np.exp(-0.5))
    a = jax.nn.sigmoid(a0_ref[...] + da)
    kx = k * kkg_ref[...]
    kk = kx * lax.rsqrt(jnp.maximum(_seg_reduce(kx * kx, m64_ref[...]), 1e-24))
    r_ref[0] = r
    k_ref[0] = k * (1.0 + (a - 1.0) * ka_ref[...])
    v_ref[0] = v
    kk_ref[0] = kk
    b_ref[0] = kk * a


def _inproj(x, prev0, tabs, wts, ts, values_transposed=False):
    b, t, _ = x.shape
    nb = prev0.shape[0]
    grid = (b, t // ts)
    tok = lambda w: pl.BlockSpec((1, ts, w), lambda bi, i: (bi, i, 0))
    in_specs = [tok(D_MODEL),
                pl.BlockSpec((1, 1, RWKV_COLS), (lambda bi, i: (bi, 0, 0)) if nb > 1 else (lambda bi, i: (0, 0, 0)))]
    in_specs += [pl.BlockSpec((ts, LANE), lambda bi, i: (i, 0)) for _ in tabs]
    in_specs += [_full_spec(w) for w in wts]
    widths = [(KV_LORA, F32), (QK_ROPE, F32), (QK_COLS, BF16), (QK_COLS, BF16), (MLA_DIM, BF16)]
    widths += [(RWKV_DIM, F32)] * 6 + [(GATE_LORA, F32)]
    out_shape = [jax.ShapeDtypeStruct((b, t, w), dt) for w, dt in widths]
    out_specs = [tok(w) for w, _ in widths]
    if values_transposed:
        out_shape[4] = jax.ShapeDtypeStruct((b, MLA_DIM, t), BF16)
        out_specs[4] = pl.BlockSpec((1, MLA_DIM, ts), lambda bi, i: (bi, 0, i))
    out_shape.append(jax.ShapeDtypeStruct((b, 1, RWKV_COLS), F32))
    out_specs.append(pl.BlockSpec((1, 1, RWKV_COLS), lambda bi, i: (bi, 0, 0)))
    return pl.pallas_call(
        functools.partial(_inproj_kernel, values_transposed=values_transposed), grid=grid, in_specs=in_specs, out_specs=out_specs, out_shape=out_shape,
        scratch_shapes=[pltpu.VMEM((1, RWKV_COLS), F32)],
        compiler_params=_params(("arbitrary", "arbitrary")), name="inproj",
    )(x, prev0, *tabs, *wts)


PAIRS = RWKV_HEADS // 2
PAIR_W = 2 * RWKV_HEAD


def _wkv_kernel(r_ref, lw_ref, k_ref, v_ref, kk_ref, b_ref, gl_ref, s0_ref,
                tri_ref, g2_ref, rk_ref, lng_ref, lnb_ref, m64mean_ref, m64sum_ref,
                rw_ref, sout_ref, state_ref):
    i = pl.program_id(1)
    tt = r_ref.shape[1]
    c = CHUNK
    nch = tt // c

    @pl.when(i == 0)
    def _():
        state_ref[...] = s0_ref[0]

    r = r_ref[0]
    lw = lw_ref[0]
    k = k_ref[0]
    v = v_ref[0]
    cum = jnp.dot(tri_ref[...], lw, preferred_element_type=F32, precision=lax.Precision.HIGHEST)
    g = jnp.exp(cum)
    gi = jnp.exp(-cum)
    gp = jnp.exp(cum - lw)
    gc_rows = [g[j * c + c - 1:(j + 1) * c, :] for j in range(nch)]
    gcb = jnp.concatenate([jnp.broadcast_to(x, (c, RWKV_DIM)) for x in gc_rows], axis=0)
    at = -(kk_ref[0] * gp)
    bt = b_ref[0] * gi
    kt = k * gi
    rt = r * g

    even = (lax.broadcasted_iota(jnp.int32, (tt, RWKV_DIM), 1) % PAIR_W) < RWKV_HEAD
    split = lambda x: (jnp.where(even, x, 0.0).astype(BF16), jnp.where(even, 0.0, x).astype(BF16))
    atm_b, rtm_b, vm_b = split(at), split(rt), split(v)
    bt_b, kt_b, v_b = bt.astype(BF16), kt.astype(BF16), v.astype(BF16)
    btc_b, ktc_b = (bt * gcb).astype(BF16), (kt * gcb).astype(BF16)

    lane = lax.broadcasted_iota(jnp.int32, (c, PAIR_W), 1)
    lo = lane < c
    eye_hi = (lane == lax.broadcasted_iota(jnp.int32, (c, PAIR_W), 0) + c).astype(F32)
    rowi = lax.broadcasted_iota(jnp.int32, (2 * c, 2 * c), 0)
    coli = lax.broadcasted_iota(jnp.int32, (2 * c, 2 * c), 1)
    colt = jnp.where(coli >= c, coli - c, coli)
    gmask = colt < jnp.where(rowi < c, rowi, rowi - c + 1)
    blk = (rowi // c) == (coli // c)
    z1 = jnp.zeros((c, PAIR_W), BF16)
    z2 = jnp.zeros((c, 2 * PAIR_W), BF16)

    items = [(j, h) for j in range(nch) for h in range(RWKV_HEADS)]
    rows = lambda j: slice(j * c, (j + 1) * c)
    lanes = lambda h: slice((h // 2) * PAIR_W, (h // 2 + 1) * PAIR_W)
    cut = lambda x, j, h: x[rows(j), lanes(h)]
    vstack = lambda a, b: jnp.concatenate([a, b], axis=0)
    hstack = lambda a, b: jnp.concatenate([a, b], axis=1)

    gb, xs = [], []
    for j, h in items:
        ar = vstack(cut(atm_b[h % 2], j, h), cut(rtm_b[h % 2], j, h))
        bk = vstack(cut(bt_b, j, h), cut(kt_b, j, h))
        gm = jnp.where(gmask, _dot_nt(ar, bk), 0.0)
        gb.append(gm.astype(BF16))
        xs.append(jnp.where(lo, gm[:c], eye_hi))
    for _ in range(int(np.log2(c)) - 1):
        xs = [_dot(x[:, :c].astype(BF16), x.astype(BF16)) + jnp.where(lo, 0.0, x) for x in xs]
    xb = [x.astype(BF16) for x in xs]
    vm = [cut(vm_b[h % 2], j, h) for j, h in items]
    akv = [_dot(gb[n][:c], vstack(z1, vm[n])) for n in range(len(items))]
    zc = [hstack(cut(atm_b[h % 2], j, h), akv[n].astype(BF16)) for n, (j, h) in enumerate(items)]
    w = [_dot(xb[n], vstack(z2, zc[n])) for n in range(len(items))]
    tu = [w[n] + _dot(xb[n], vstack(w[n].astype(BF16), z2)) for n in range(len(items))]
    ry = [_dot(gb[n][c:], vstack(tu[n].astype(BF16), hstack(z1, vm[n]))) for n in range(len(items))]

    reff, yv, pm, qm = {}, {}, {}, {}
    for j in range(nch):
        for p in range(PAIRS):
            n0 = j * RWKV_HEADS + 2 * p
            tu_p = tu[n0] + tu[n0 + 1]
            ry_p = ry[n0] + ry[n0 + 1]
            h = 2 * p
            reff[j, p] = (cut(rt, j, h) + ry_p[:, :PAIR_W]).astype(BF16)
            yv[j, p] = ry_p[:, PAIR_W:]
            pm[j, p] = jnp.where(blk, _dot_tn(tu_p[:, :PAIR_W].astype(BF16), cut(btc_b, j, h)), 0.0).astype(BF16)
            qm[j, p] = jnp.where(blk, _dot_tn(vstack(tu_p[:, PAIR_W:].astype(BF16), cut(v_b, j, h)),
                                              vstack(cut(btc_b, j, h), cut(ktc_b, j, h))), 0.0)

    st = [state_ref[p] for p in range(PAIRS)]
    ys = []
    for j in range(nch):
        sb = [s.astype(BF16) for s in st]
        ys.append(jnp.concatenate([_dot_nt(reff[j, p], sb[p]) + yv[j, p] for p in range(PAIRS)], axis=1))
        st = [st[p] * gc_rows[j][:, p * PAIR_W:(p + 1) * PAIR_W] + _dot(sb[p], pm[j, p]) + qm[j, p]
              for p in range(PAIRS)]
    for p in range(PAIRS):
        state_ref[p] = st[p]
    sout_ref[0] = state_ref[...]

    y = ys[0] if nch == 1 else jnp.concatenate(ys, axis=0)
    mean = _seg_reduce(y, m64mean_ref[...])
    d = y - mean
    var = _seg_reduce(d * d, m64mean_ref[...])
    yn = d * lax.rsqrt(var + LNX_EPS) * lng_ref[...] + lnb_ref[...]
    bonus = _seg_reduce(r * k * rk_ref[...], m64sum_ref[...]) * v
    gate = _dot(jax.nn.sigmoid(gl_ref[0]).astype(BF16), g2_ref[...])
    rw_ref[0] = ((yn + bonus) * gate).astype(rw_ref.dtype)


def _state_to_pairs(s):
    z = jnp.zeros_like(s[:, 0::2])
    top = jnp.concatenate([s[:, 0::2], z], axis=-1)
    bot = jnp.concatenate([z, s[:, 1::2]], axis=-1)
    return jnp.concatenate([top, bot], axis=2)


def _pairs_to_state(sp):
    b = sp.shape[0]
    s = jnp.stack([sp[:, :, :RWKV_HEAD, :RWKV_HEAD], sp[:, :, RWKV_HEAD:, RWKV_HEAD:]], axis=2)
    return s.reshape(b, RWKV_HEADS, RWKV_HEAD, RWKV_HEAD)


def _wkv(seqs, s0, consts, tt):
    b, t, _ = seqs[0].shape
    nb = s0.shape[0]
    tok = lambda w: pl.BlockSpec((1, tt, w), lambda bi, i: (bi, i, 0))
    st_shape = (1, PAIRS, PAIR_W, PAIR_W)
    in_specs = [tok(RWKV_DIM)] * 6 + [tok(GATE_LORA)]
    in_specs.append(pl.BlockSpec(st_shape, (lambda bi, i: (bi, 0, 0, 0)) if nb > 1 else (lambda bi, i: (0, 0, 0, 0))))
    tri = jnp.kron(jnp.eye(tt // CHUNK, dtype=F32), jnp.tril(jnp.ones((CHUNK, CHUNK), F32)))
    consts = (tri,) + tuple(consts)
    in_specs += [_full_spec(w) for w in consts]
    rw, sp = pl.pallas_call(
        _wkv_kernel, grid=(b, t // tt),
        in_specs=in_specs,
        out_specs=[tok(RWKV_DIM), pl.BlockSpec(st_shape, lambda bi, i: (bi, 0, 0, 0))],
        out_shape=[jax.ShapeDtypeStruct((b, t, RWKV_DIM), BF16),
                   jax.ShapeDtypeStruct((b,) + st_shape[1:], F32)],
        scratch_shapes=[pltpu.VMEM(st_shape[1:], F32)],
        compiler_params=_params(("arbitrary", "arbitrary")), name="wkv",
    )(*seqs, _state_to_pairs(s0), *consts)
    return rw, _pairs_to_state(sp)


def _pad_chunk(seqs):
    t = seqs[0].shape[1]
    pad = (-t) % CHUNK
    return [jnp.pad(a, ((0, 0), (0, pad), (0, 0))) for a in seqs]


NPAIR = MLA_HEADS // 2


def _attn_step(qh, kt, vt, mask, first, m_ref, l_ref, acc_ref):
    n = vt(0).shape[0]
    tq = acc_ref.shape[1]
    even = lax.broadcasted_iota(jnp.int32, (n, LANE), 1) < V_DIM
    lane_q = lax.broadcasted_iota(jnp.int32, (tq, LANE), 1) < V_DIM
    s = [_dot_nt(qh(h), kt(h)) for h in range(MLA_HEADS)]
    if mask is not None:
        s = [jnp.where(mask, x, NEG) for x in s]
    ps, alphas = [], []
    for h in range(MLA_HEADS):
        mx = jnp.max(s[h], axis=-1, keepdims=True)
        if first:
            m_new = mx
        else:
            m_old = m_ref[h]
            m_new = jnp.maximum(m_old, mx)
            alphas.append(jnp.exp(m_old - m_new))
        pr = jnp.exp(s[h] - m_new)
        sm = jnp.sum(pr, axis=-1, keepdims=True)
        l_ref[h] = sm if first else alphas[h] * l_ref[h] + sm
        m_ref[h] = m_new
        ps.append(pr.astype(BF16))
    for p in range(NPAIR):
        v = vt(p)
        zero = jnp.zeros_like(v)
        vcat = jnp.concatenate([jnp.where(even, v, zero), jnp.where(even, zero, v)], axis=0)
        pv = _dot(jnp.concatenate([ps[2 * p], ps[2 * p + 1]], axis=1), vcat)
        if first:
            acc_ref[p] = pv
        else:
            acc_ref[p] = jnp.where(lane_q, alphas[2 * p], alphas[2 * p + 1]) * acc_ref[p] + pv


def _attn_finish(l_ref, acc_ref, g):
    tq = acc_ref.shape[1]
    lane_q = lax.broadcasted_iota(jnp.int32, (tq, LANE), 1) < V_DIM
    o = jnp.concatenate([acc_ref[p] / jnp.where(lane_q, l_ref[2 * p], l_ref[2 * p + 1])
                         for p in range(NPAIR)], axis=1)
    return o * lax.rsqrt(jnp.mean(o * o, axis=-1, keepdims=True) + EPS) * g


def _attn_scratch(tq):
    return [pltpu.VMEM((MLA_HEADS, tq, 1), F32), pltpu.VMEM((MLA_HEADS, tq, 1), F32),
            pltpu.VMEM((NPAIR, tq, LANE), F32)]


_hq = lambda h: slice(h * HEAD_GROUP, (h + 1) * HEAD_GROUP)
_hv = lambda p: slice(p * LANE, (p + 1) * LANE)


L_ROWS = 16
ACC_ROWS = V_DIM + L_ROWS


def _attn_step_t(qh, kt, vt, mask, first, m_ref, acc_ref):
    n = vt(0).shape[1]
    ones = jnp.ones((L_ROWS, n), BF16)
    s = [_dot_nt(kt(h), qh(h)) for h in range(MLA_HEADS)]
    if mask is not None:
        s = [jnp.where(mask, x, NEG) for x in s]
    ps, alphas = [], []
    for h in range(MLA_HEADS):
        mx = jnp.max(s[h], axis=0, keepdims=True)
        if first:
            m_new = mx
        else:
            m_old = m_ref[h]
            m_new = jnp.maximum(m_old, mx)
            alphas.append(jnp.exp(m_old - m_new))
        m_ref[h] = m_new
        ps.append(jnp.exp(s[h] - m_new).astype(BF16))
    for h in range(MLA_HEADS):
        pv = _dot(jnp.concatenate([vt(h), ones], axis=0), ps[h])
        acc_ref[h] = pv if first else alphas[h] * acc_ref[h] + pv


def _attn_prompt_kernel(q_ref, k_ref, vt_ref, km_ref, vtm_ref, g_ref, o_ref, m_ref, acc_ref, *, tk):
    i = pl.program_id(1)
    tq = q_ref.shape[1]
    assert tq == tk
    st = (m_ref, acc_ref)
    qh = lambda h: q_ref[0, :, _hq(h)]
    hd = lambda h: slice(h * V_DIM, (h + 1) * V_DIM)

    meta_mask = lax.broadcasted_iota(jnp.int32, (LANE, tq), 0) < N_META
    _attn_step_t(qh, lambda h: km_ref[:, _hq(h)], lambda h: vtm_ref[hd(h), :], meta_mask, True, *st)

    def body(t, carry):
        ks = pl.ds(pl.multiple_of(t * tk, tk), tk)
        _attn_step_t(qh, lambda h: k_ref[0, ks, _hq(h)], lambda h: vt_ref[0, hd(h), ks], None, False, *st)
        return carry

    lax.fori_loop(0, i, body, 0)
    krow = lax.broadcasted_iota(jnp.int32, (tk, tq), 0) // CHUNK
    qcol = lax.broadcasted_iota(jnp.int32, (tk, tq), 1) // CHUNK
    ks = pl.ds(pl.multiple_of(i * tk, tk), tk)
    _attn_step_t(qh, lambda h: k_ref[0, ks, _hq(h)], lambda h: vt_ref[0, hd(h), ks], krow <= qcol, False, *st)

    o_t = jnp.concatenate([acc_ref[h, :V_DIM, :] / acc_ref[h, V_DIM:V_DIM + 1, :] for h in range(MLA_HEADS)],
                          axis=0)
    o_t = o_t * lax.rsqrt(jnp.mean(o_t * o_t, axis=0, keepdims=True) + EPS)
    o_ref[0] = (o_t.T * g_ref[...]).astype(o_ref.dtype)


def _attn_prompt(q, kc, vt, kmeta, vtmeta, g, tq):
    b, s, _ = q.shape
    return pl.pallas_call(
        functools.partial(_attn_prompt_kernel, tk=tq), grid=(b, s // tq),
        in_specs=[pl.BlockSpec((1, tq, QK_COLS), lambda bi, i: (bi, i, 0)),
                  pl.BlockSpec((1, s, QK_COLS), lambda bi, i: (bi, 0, 0)),
                  pl.BlockSpec((1, MLA_DIM, s), lambda bi, i: (bi, 0, 0)),
                  _full_spec(kmeta), _full_spec(vtmeta), _full_spec(g)],
        out_specs=pl.BlockSpec((1, tq, MLA_DIM), lambda bi, i: (bi, i, 0)),
        out_shape=jax.ShapeDtypeStruct((b, s, MLA_DIM), BF16),
        scratch_shapes=[pltpu.VMEM((MLA_HEADS, 1, tq), F32), pltpu.VMEM((MLA_HEADS, ACC_ROWS, tq), F32)],
        compiler_params=_params(("arbitrary", "arbitrary")), name="attn_prompt",
    )(q, kc, vt, kmeta, vtmeta, g)


def _attn_sample_kernel(q_ref, kn_ref, vn_ref, cc_ref, ck_ref, wkk_ref, gkn_ref, mk_ref,
                        erep_ref, wv_ref, g_ref, o_ref, m_ref, l_ref, acc_ref, *, tk, past):
    tq = q_ref.shape[1]
    st = (m_ref, l_ref, acc_ref)
    qh = lambda h: q_ref[0, :, _hq(h)]
    qchunk = (past + lax.broadcasted_iota(jnp.int32, (tq, 1), 0)) // CHUNK
    kchunk = (past + lax.broadcasted_iota(jnp.int32, (1, tq), 1)) // CHUNK
    _attn_step(qh, lambda h: kn_ref[0, :, _hq(h)], lambda p: vn_ref[0, :, _hv(p)], kchunk <= qchunk, True, *st)

    def body(t, carry):
        ks = pl.ds(pl.multiple_of(t * tk, tk), tk)
        cb = cc_ref[0, ks, :].astype(BF16)
        kraw = _dot(cb, wkk_ref[...])
        kn = kraw * lax.rsqrt(_seg_reduce(kraw * kraw, mk_ref[...]) + EPS) * gkn_ref[...]
        kc = (kn + _dot(ck_ref[0, ks, :].astype(BF16), erep_ref[...])).astype(BF16)
        vc = _dot(cb, wv_ref[...]).astype(BF16)
        kch = (t * tk + lax.broadcasted_iota(jnp.int32, (1, tk), 1)) // CHUNK
        _attn_step(qh, lambda h: kc[:, _hq(h)], lambda p: vc[:, _hv(p)], kch <= qchunk, False, *st)
        return carry

    lax.fori_loop(0, past // tk, body, 0)
    o_ref[0] = _attn_finish(l_ref, acc_ref, g_ref[...]).astype(o_ref.dtype)


def _attn_sample(q, kn, vn, cache_c, cache_kr, wts, g, tk):
    b, t, _ = q.shape
    past = cache_c.shape[1]
    bspec = lambda shp: pl.BlockSpec((1,) + shp, lambda bi: (bi, 0, 0))
    return pl.pallas_call(
        functools.partial(_attn_sample_kernel, tk=tk, past=past), grid=(b,),
        in_specs=[bspec((t, QK_COLS)), bspec((t, QK_COLS)), bspec((t, MLA_DIM)),
                  bspec((past, KV_LORA)), bspec((past, QK_ROPE))]
                 + [_full_spec(w) for w in wts] + [_full_spec(g)],
        out_specs=bspec((t, MLA_DIM)),
        out_shape=jax.ShapeDtypeStruct((b, t, MLA_DIM), BF16),
        scratch_shapes=_attn_scratch(t),
        compiler_params=_params(("arbitrary",)), name="attn_sample",
    )(q, kn, vn, cache_c, cache_kr, *wts, g)


def _tail_kernel(x_ref, at_ref, rw_ref, wo_ref, gf_ref, wg_ref, wu_ref, wd_ref, o_ref, *, fc):
    mix = jnp.concatenate([at_ref[...], rw_ref[...]], axis=1)
    h = x_ref[...] + _dot(mix, wo_ref[...])
    u = (h * lax.rsqrt(jnp.mean(h * h, axis=-1, keepdims=True) + EPS) * gf_ref[...]).astype(BF16)
    acc = h
    for j in range(D_FF // fc):
        cs = slice(j * fc, (j + 1) * fc)
        gt = _dot(u, wg_ref[:, cs])
        up = _dot(u, wu_ref[:, cs])
        act = (gt * jax.nn.sigmoid(gt) * up).astype(BF16)
        acc = acc + _dot(act, wd_ref[cs, :])
    o_ref[...] = acc


def _tail(x, attn, rw, wts, ts):
    n = x.shape[0]
    tok = lambda w: pl.BlockSpec((ts, w), lambda i: (i, 0))
    wspec = lambda w: pl.BlockSpec(w.shape, lambda i: (0,) * w.ndim, pipeline_mode=pl.Buffered(1))
    return pl.pallas_call(
        functools.partial(_tail_kernel, fc=MXU_N), grid=(n // ts,),
        in_specs=[tok(D_MODEL), tok(MLA_DIM), tok(RWKV_DIM)] + [wspec(w) for w in wts],
        out_specs=tok(D_MODEL),
        out_shape=jax.ShapeDtypeStruct((n, D_MODEL), F32),
        compiler_params=_params(("arbitrary",)), name="tail",
    )(x, attn, rw, *wts)


def _block_matrix(size, seg_of, scale_of):
    seg = np.array([seg_of(i) for i in range(size)])
    m = np.zeros((size, size), np.float32)
    for i in range(size):
        if seg[i] >= 0:
            m[i, seg == seg[i]] = scale_of(i)
    return jnp.asarray(m, BF16)


def _rope_tables(pos):
    half = QK_ROPE // 2
    inv = ROPE_BASE ** (-jnp.arange(half, dtype=F32) / half)
    ang = pos.astype(F32)[:, None] * inv[None, :]
    cos, sin = jnp.cos(ang), jnp.sin(ang)
    t = pos.shape[0]
    cc = jnp.concatenate([cos, cos], axis=1)
    ss = jnp.concatenate([-sin, sin], axis=1)
    zq = jnp.zeros((t, LANE - QK_NOPE - QK_ROPE), F32)
    cosq = jnp.concatenate([jnp.ones((t, QK_NOPE), F32), cc, zq], axis=1)
    sinq = jnp.concatenate([jnp.zeros((t, QK_NOPE), F32), ss, zq], axis=1)
    zk = jnp.zeros((t, LANE - QK_ROPE), F32)
    return cosq, sinq, jnp.concatenate([cc, zk], axis=1), jnp.concatenate([ss, zk], axis=1)


def _prep_weights(norm_mix_g, w_in, q_norm_g, w_q_up, kv_norm_g, w_kv_up, qn_nope_g, qn_rope_g,
                  kn_nope_g, kn_rope_g, mu_shift, w0, w2, a0, a2, k_k, k_a):
    half = QK_ROPE // 2
    swap = np.concatenate([np.arange(half, QK_ROPE), np.arange(half)])
    row = lambda v: v.astype(F32).reshape(1, -1)
    pad_to = lambda v, n: jnp.concatenate([v, jnp.zeros(v.shape[:-1] + (n - v.shape[-1],), v.dtype)], axis=-1)

    mla_cols = Q_LORA + KV_LORA + QK_ROPE
    w_kr = w_in[:, Q_LORA + KV_LORA:mla_cols]
    win = jnp.concatenate([pad_to(jnp.concatenate([w_in[:, :mla_cols], w_kr[:, swap]], axis=1), MLA_EXT),
                           w_in[:, mla_cols:]], axis=1).astype(BF16)

    qh = w_q_up.reshape(Q_LORA, MLA_HEADS, QK_NOPE + QK_ROPE)
    zq = jnp.zeros((Q_LORA, MLA_HEADS, QK_NOPE), w_q_up.dtype)
    wqa = pad_to(qh, HEAD_GROUP).reshape(Q_LORA, QK_COLS).astype(BF16)
    wqb = pad_to(jnp.concatenate([zq, qh[:, :, QK_NOPE:][:, :, swap]], axis=2), HEAD_GROUP)
    wqb = wqb.reshape(Q_LORA, QK_COLS).astype(BF16)
    gqa = jnp.tile(pad_to(jnp.concatenate([qn_nope_g, qn_rope_g]), HEAD_GROUP), MLA_HEADS)
    gqb = jnp.tile(pad_to(jnp.concatenate([jnp.zeros_like(qn_nope_g), qn_rope_g[swap]]), HEAD_GROUP), MLA_HEADS)

    def seg_q(i):
        j = i % HEAD_GROUP
        base = (i // HEAD_GROUP) * 2
        return base if j < QK_NOPE else (base + 1 if j < QK_NOPE + QK_ROPE else -1)
    mq = _block_matrix(MXU_N, seg_q, lambda i: 1.0 / (QK_NOPE if i % HEAD_GROUP < QK_NOPE else QK_ROPE))
    mk = _block_matrix(MXU_N, lambda i: i // HEAD_GROUP if i % HEAD_GROUP < QK_NOPE else -1,
                       lambda i: 1.0 / QK_NOPE)
    m64sum = _block_matrix(MXU_N, lambda i: i // RWKV_HEAD, lambda i: 1.0)
    m64mean = _block_matrix(MXU_N, lambda i: i // RWKV_HEAD, lambda i: 1.0 / RWKV_HEAD)

    kvh = w_kv_up.reshape(KV_LORA, MLA_HEADS, QK_NOPE + V_DIM)
    wkk = pad_to(kvh[:, :, :QK_NOPE], HEAD_GROUP).reshape(KV_LORA, QK_COLS).astype(BF16)
    wv = kvh[:, :, QK_NOPE:].reshape(KV_LORA, MLA_DIM).astype(BF16)
    gkn = jnp.tile(pad_to(kn_nope_g, HEAD_GROUP), MLA_HEADS)
    erep_np = np.zeros((LANE, QK_COLS), np.float32)
    for h in range(MLA_HEADS):
        for j in range(QK_ROPE):
            erep_np[j, h * HEAD_GROUP + QK_NOPE + j] = 1.0
    erep = jnp.asarray(erep_np, BF16)
    gkr = pad_to(kn_rope_g, LANE)
    gkrs = pad_to(kn_rope_g[swap], LANE)

    w2e = jnp.concatenate([w2, jnp.zeros((AAA_LORA, RWKV_DIM), w2.dtype)], axis=0).astype(BF16)
    a2e = jnp.concatenate([jnp.zeros((DECAY_LORA, RWKV_DIM), a2.dtype), a2], axis=0).astype(BF16)

    inproj_w = [row(norm_mix_g), win, row(q_norm_g), wqa, wqb, row(gqa), row(gqb), mq,
                row(kv_norm_g), row(gkr), row(gkrs), wkk, row(gkn), mk, erep, wv, wv.T,
                row(mu_shift), row(w0), w2e, row(a0), a2e, row(k_k), row(k_a), m64sum]
    sample_w = [wkk, row(gkn), mk, erep[:QK_ROPE], wv]
    return inproj_w, sample_w, m64mean, m64sum


def kernel(x_prompt, x_sample, cache_kv_latent, cache_k_rope, state_wkv, state_shift, meta_tokens, norm_mix_g, w_in, q_norm_g, w_q_up, kv_norm_g, w_kv_up, qn_nope_g, qn_rope_g, kn_nope_g, kn_rope_g, attn_out_g, mu_shift, w0, w2, a0, a2, g2, k_k, k_a, r_k, lnx_g, lnx_b, w_out, norm_ffn_g, w_gate, w_up, w_down):
    b, s, _ = x_prompt.shape
    db, ds, _ = x_sample.shape
    past = cache_kv_latent.shape[1]
    row = lambda v: v.astype(F32).reshape(1, -1)

    inproj_w, sample_w, m64mean, m64sum = _prep_weights(
        norm_mix_g, w_in, q_norm_g, w_q_up, kv_norm_g, w_kv_up, qn_nope_g, qn_rope_g,
        kn_nope_g, kn_rope_g, mu_shift, w0, w2, a0, a2, k_k, k_a)
    wkv_consts = (g2.astype(BF16), row(r_k), row(lnx_g), row(lnx_b), m64mean, m64sum)
    tail_w = [w_out.astype(BF16), row(norm_ffn_g), w_gate.astype(BF16), w_up.astype(BF16), w_down.astype(BF16)]
    g_attn = row(attn_out_g)

    ts = min(512, s)
    meta = _inproj(meta_tokens.astype(F32)[None], jnp.zeros((1, 1, RWKV_COLS), F32),
                   _rope_tables(jnp.arange(N_META)), inproj_w, N_META, values_transposed=True)
    c_m, kr_m, _, kc_m, vt_m = meta[:5]
    _, s_meta = _wkv(_pad_chunk(meta[5:12]), jnp.zeros((1, RWKV_HEADS, RWKV_HEAD, RWKV_HEAD), F32),
                     wkv_consts, CHUNK)

    pr = _inproj(x_prompt, meta[12], _rope_tables(N_META + jnp.arange(s)), inproj_w, ts, values_transposed=True)
    c_p, kr_p, q_p, kc_p, vt_p = pr[:5]
    rw_p, wkv_p = _wkv(pr[5:12], s_meta, wkv_consts, 2 * CHUNK)
    kc_m = jnp.pad(kc_m[0], ((0, LANE - N_META), (0, 0)))
    vt_m = jnp.pad(vt_m[0], ((0, 0), (0, LANE - N_META)))
    attn_p = _attn_prompt(q_p, kc_p, vt_p, kc_m, vt_m, g_attn, min(256, s))
    y_prompt = _tail(x_prompt.reshape(b * s, D_MODEL), attn_p.reshape(b * s, MLA_DIM),
                     rw_p.reshape(b * s, RWKV_DIM), tail_w, ts).reshape(b, s, D_MODEL)
    kv_latent_p = jnp.concatenate([jnp.broadcast_to(c_m, (b, N_META, KV_LORA)), c_p], axis=1)
    k_rope_p = jnp.concatenate([jnp.broadcast_to(kr_m, (b, N_META, QK_ROPE)), kr_p], axis=1)

    sm = _inproj(x_sample, state_shift.astype(F32)[:, None, :], _rope_tables(past + jnp.arange(ds)),
                 inproj_w, ds)
    c_s, kr_s, q_s, kc_s, vc_s = sm[:5]
    rw_s, wkv_s = _wkv(_pad_chunk(sm[5:12]), state_wkv.astype(F32), wkv_consts, CHUNK)
    rw_s = rw_s[:, :ds]
    attn_s = _attn_sample(q_s, kc_s, vc_s, cache_kv_latent, cache_k_rope, sample_w, g_attn, min(512, past))
    y_sample = _tail(x_sample.reshape(db * ds, D_MODEL), attn_s.reshape(db * ds, MLA_DIM),
                     rw_s.reshape(db * ds, RWKV_DIM), tail_w, min(512, db * ds)).reshape(db, ds, D_MODEL)

    return (y_prompt, y_sample, kv_latent_p, k_rope_p, wkv_p, pr[12][:, 0, :],
            c_s, kr_s, wkv_s, sm[12][:, 0, :])
```

```python
import functools

import numpy as np
import jax
import jax.numpy as jnp
from jax import lax
from jax.experimental import pallas as pl
from jax.experimental.pallas import tpu as pltpu

F32 = jnp.float32
BF16 = jnp.bfloat16

D_MODEL = 1024
CHUNK = 64
N_META = 16
EPS = 1e-6
NEG = -1e30
MLA_HEADS = 8
QK_NOPE = 64
QK_ROPE = 32
V_DIM = 64
Q_LORA = 256
KV_LORA = 128
ROPE_BASE = 10000.0
ATTN_SCALE = (QK_NOPE + QK_ROPE) ** -0.5
MLA_DIM = MLA_HEADS * V_DIM
RWKV_HEADS = 8
RWKV_HEAD = 64
RWKV_DIM = RWKV_HEADS * RWKV_HEAD
DECAY_LORA = 64
AAA_LORA = 64
GATE_LORA = 128
LNX_EPS = 64e-5
RWKV_COLS = 3 * RWKV_DIM + DECAY_LORA + AAA_LORA + GATE_LORA
D_FF = -(-8 * D_MODEL // (3 * 256)) * 256

LANE = 128
MXU_N = 256
HEAD_GROUP = LANE
QK_COLS = MLA_HEADS * HEAD_GROUP
MLA_EXT = 512
IN_EXT = MLA_EXT + RWKV_COLS
VMEM_LIMIT = 56 * 1024 * 1024


def _dot(a, b):
    return jnp.dot(a, b, preferred_element_type=F32)


def _dot_nt(a, b):
    return lax.dot_general(a, b, (((1,), (1,)), ((), ())), preferred_element_type=F32)


def _dot_tn(a, b):
    return lax.dot_general(a, b, (((0,), (0,)), ((), ())), preferred_element_type=F32)


def _seg_reduce(x, m, terms=2):
    parts = [x.astype(BF16)]
    if terms == 2:
        parts.append((x - parts[0].astype(F32)).astype(BF16))
    outs = []
    for j in range(x.shape[1] // MXU_N):
        sl = slice(j * MXU_N, (j + 1) * MXU_N)
        outs.append(sum(_dot(part[:, sl], m) for part in parts))
    return outs[0] if len(outs) == 1 else jnp.concatenate(outs, axis=1)


def _full_spec(arr):
    nd = arr.ndim
    return pl.BlockSpec(arr.shape, lambda *_: (0,) * nd)


def _params(sem):
    return pltpu.CompilerParams(dimension_semantics=sem, vmem_limit_bytes=VMEM_LIMIT)


def _inproj_kernel(x_ref, prev0_ref, cosq_ref, sinq_ref, cosk_ref, sink_ref,
                   gmix_ref, win_ref, gql_ref, wqa_ref, wqb_ref, gqa_ref, gqb_ref, mq_ref,
                   gkv_ref, gkr_ref, gkrs_ref, wkk_ref, gkn_ref, mk_ref, erep_ref, wv_ref, wvt_ref,
                   mu_ref, w0_ref, w2_ref, a0_ref, a2_ref, kkg_ref, ka_ref, m64_ref,
                   c_ref, kr_ref, q_ref, kc_ref, vc_ref,
                   r_ref, lw_ref, k_ref, v_ref, kk_ref, b_ref, gl_ref, shift_ref,
                   carry_ref, *, values_transposed):
    i = pl.program_id(1)
    ts = x_ref.shape[1]

    x = x_ref[0]
    xn = x * lax.rsqrt(jnp.mean(x * x, axis=-1, keepdims=True) + EPS) * gmix_ref[...]
    p = _dot(xn.astype(BF16), win_ref[...])

    pq = p[:, :Q_LORA]
    ql = pq * lax.rsqrt(jnp.mean(pq * pq, axis=-1, keepdims=True) + EPS) * gql_ref[...]
    ql = ql.astype(BF16)
    qa = _dot(ql, wqa_ref[...])
    qb = _dot(ql, wqb_ref[...])
    qs = lax.rsqrt(_seg_reduce(qa * qa, mq_ref[...], terms=1) + EPS)
    cosq = jnp.concatenate([cosq_ref[...]] * MLA_HEADS, axis=1)
    sinq = jnp.concatenate([sinq_ref[...]] * MLA_HEADS, axis=1)
    q = qs * (qa * gqa_ref[...] * cosq + qb * gqb_ref[...] * sinq) * ATTN_SCALE
    q_ref[0] = q.astype(BF16)

    pc = p[:, Q_LORA:Q_LORA + KV_LORA]
    c = pc * lax.rsqrt(jnp.mean(pc * pc, axis=-1, keepdims=True) + EPS) * gkv_ref[...]
    c_ref[0] = c
    cb = c.astype(BF16)

    pk = p[:, Q_LORA + KV_LORA:MLA_EXT]
    lane = lax.broadcasted_iota(jnp.int32, pk.shape, 1)
    ssk = jnp.sum(jnp.where(lane < QK_ROPE, pk * pk, 0.0), axis=-1, keepdims=True)
    sk = lax.rsqrt(ssk * (1.0 / QK_ROPE) + EPS)
    pk_sw = pltpu.roll(pk, LANE - QK_ROPE, axis=1)
    kr = sk * (pk * gkr_ref[...] * cosk_ref[...] + pk_sw * gkrs_ref[...] * sink_ref[...])
    kr_ref[0] = kr[:, :QK_ROPE]

    kraw = _dot(cb, wkk_ref[...])
    kn = kraw * lax.rsqrt(_seg_reduce(kraw * kraw, mk_ref[...], terms=1) + EPS) * gkn_ref[...]
    kc_ref[0] = (kn + _dot(kr.astype(BF16), erep_ref[...])).astype(BF16)
    if values_transposed:
        vc_ref[0] = _dot_nt(wvt_ref[...], cb).astype(BF16)
    else:
        vc_ref[0] = _dot(cb, wv_ref[...]).astype(BF16)

    prw = p[:, MLA_EXT:]

    @pl.when(i == 0)
    def _():
        carry_ref[...] = prev0_ref[0]

    row = lax.broadcasted_iota(jnp.int32, (ts, 1), 0)
    shifted = jnp.where(row == 0, carry_ref[...], pltpu.roll(prw, 1, axis=0))
    last = prw[ts - 1:ts, :]
    carry_ref[...] = last
    shift_ref[0] = last
    xm = prw + (shifted - prw) * mu_ref[...]

    o = RWKV_DIM
    r = xm[:, :o]
    k = xm[:, o:2 * o]
    v = xm[:, 2 * o:3 * o]
    wa = xm[:, 3 * o:3 * o + DECAY_LORA + AAA_LORA]
    gl_ref[0] = xm[:, 3 * o + DECAY_LORA + AAA_LORA:]
    dw = _dot(jnp.tanh(wa).astype(BF16), w2_ref[...])
    da = _dot(wa.astype(BF16), a2_ref[...])
    lw_ref[0] = -jax.nn.sigmoid(w0_ref[...] + dw) * float(np.exp(-0.5))
    a = jax.nn.sigmoid(a0_ref[...] + da)
    kx = k * kkg_ref[...]
    kk = kx * lax.rsqrt(jnp.maximum(_seg_reduce(kx * kx, m64_ref[...], terms=1), 1e-24))
    r_ref[0] = r
    k_ref[0] = k * (1.0 + (a - 1.0) * ka_ref[...])
    v_ref[0] = v
    kk_ref[0] = kk
    b_ref[0] = kk * a


def _inproj(x, prev0, tabs, wts, ts, values_transposed=False):
    b, t, _ = x.shape
    nb = prev0.shape[0]
    grid = (b, t // ts)
    tok = lambda w: pl.BlockSpec((1, ts, w), lambda bi, i: (bi, i, 0))
    in_specs = [tok(D_MODEL),
                pl.BlockSpec((1, 1, RWKV_COLS), (lambda bi, i: (bi, 0, 0)) if nb > 1 else (lambda bi, i: (0, 0, 0)))]
    in_specs += [pl.BlockSpec((ts, LANE), lambda bi, i: (i, 0)) for _ in tabs]
    in_specs += [_full_spec(w) for w in wts]
    widths = [(KV_LORA, F32), (QK_ROPE, F32), (QK_COLS, BF16), (QK_COLS, BF16), (MLA_DIM, BF16)]
    widths += [(RWKV_DIM, F32)] * 6 + [(GATE_LORA, F32)]
    out_shape = [jax.ShapeDtypeStruct((b, t, w), dt) for w, dt in widths]
    out_specs = [tok(w) for w, _ in widths]
    if values_transposed:
        out_shape[4] = jax.ShapeDtypeStruct((b, MLA_DIM, t), BF16)
        out_specs[4] = pl.BlockSpec((1, MLA_DIM, ts), lambda bi, i: (bi, 0, i))
    out_shape.append(jax.ShapeDtypeStruct((b, 1, RWKV_COLS), F32))
    out_specs.append(pl.BlockSpec((1, 1, RWKV_COLS), lambda bi, i: (bi, 0, 0)))
    return pl.pallas_call(
        functools.partial(_inproj_kernel, values_transposed=values_transposed), grid=grid, in_specs=in_specs, out_specs=out_specs, out_shape=out_shape,
        scratch_shapes=[pltpu.VMEM((1, RWKV_COLS), F32)],
        compiler_params=_params(("arbitrary", "arbitrary")), name="inproj",
    )(x, prev0, *tabs, *wts)


PAIRS = RWKV_HEADS // 2
PAIR_W = 2 * RWKV_HEAD


def _wkv_kernel(r_ref, lw_ref, k_ref, v_ref, kk_ref, b_ref, gl_ref, s0_ref,
                tri_ref, g2_ref, rk_ref, lng_ref, lnb_ref, m64mean_ref, m64sum_ref,
                rw_ref, sout_ref, state_ref):
    i = pl.program_id(1)
    tt = r_ref.shape[1]
    c = CHUNK
    nch = tt // c

    @pl.when(i == 0)
    def _():
        state_ref[...] = s0_ref[0]

    r = r_ref[0]
    lw = lw_ref[0]
    k = k_ref[0]
    v = v_ref[0]
    tri = tri_ref[...]
    l1 = lw.astype(BF16)
    e1 = lw - l1.astype(F32)
    l2 = e1.astype(BF16)
    l3 = (e1 - l2.astype(F32)).astype(BF16)
    cum = _dot(tri, l1) + _dot(tri, l2) + _dot(tri, l3)
    g = jnp.exp(cum)
    gi = jnp.exp(-cum)
    gp = jnp.exp(cum - lw)
    gc_rows = [g[j * c + c - 1:(j + 1) * c, :] for j in range(nch)]
    gcb = jnp.concatenate([jnp.broadcast_to(x, (c, RWKV_DIM)) for x in gc_rows], axis=0)
    at = -(kk_ref[0] * gp)
    bt = b_ref[0] * gi
    kt = k * gi
    rt = r * g

    even = (lax.broadcasted_iota(jnp.int32, (tt, RWKV_DIM), 1) % PAIR_W) < RWKV_HEAD
    split = lambda x: (jnp.where(even, x, 0.0).astype(BF16), jnp.where(even, 0.0, x).astype(BF16))
    atm_b, rtm_b, vm_b = split(at), split(rt), split(v)
    bt_b, kt_b, v_b = bt.astype(BF16), kt.astype(BF16), v.astype(BF16)
    btc_b, ktc_b = (bt * gcb).astype(BF16), (kt * gcb).astype(BF16)

    lane = lax.broadcasted_iota(jnp.int32, (c, PAIR_W), 1)
    lo = lane < c
    eye_hi = (lane == lax.broadcasted_iota(jnp.int32, (c, PAIR_W), 0) + c).astype(F32)
    rowi = lax.broadcasted_iota(jnp.int32, (2 * c, 2 * c), 0)
    coli = lax.broadcasted_iota(jnp.int32, (2 * c, 2 * c), 1)
    colt = jnp.where(coli >= c, coli - c, coli)
    gmask = colt < jnp.where(rowi < c, rowi, rowi - c + 1)
    blk = (rowi // c) == (coli // c)
    z1 = jnp.zeros((c, PAIR_W), BF16)
    z2 = jnp.zeros((c, 2 * PAIR_W), BF16)

    items = [(j, h) for j in range(nch) for h in range(RWKV_HEADS)]
    rows = lambda j: slice(j * c, (j + 1) * c)
    lanes = lambda h: slice((h // 2) * PAIR_W, (h // 2 + 1) * PAIR_W)
    cut = lambda x, j, h: x[rows(j), lanes(h)]
    vstack = lambda a, b: jnp.concatenate([a, b], axis=0)
    hstack = lambda a, b: jnp.concatenate([a, b], axis=1)

    gb, xs = [], []
    for j, h in items:
        ar = vstack(cut(atm_b[h % 2], j, h), cut(rtm_b[h % 2], j, h))
        bk = vstack(cut(bt_b, j, h), cut(kt_b, j, h))
        gm = jnp.where(gmask, _dot_nt(ar, bk), 0.0)
        gb.append(gm.astype(BF16))
        xs.append(jnp.where(lo, gm[:c], eye_hi))
    for _ in range(int(np.log2(c)) - 1):
        xs = [_dot(x[:, :c].astype(BF16), x.astype(BF16)) + jnp.where(lo, 0.0, x) for x in xs]
    xb = [x.astype(BF16) for x in xs]
    vm = [cut(vm_b[h % 2], j, h) for j, h in items]
    akv = [_dot(gb[n][:c], vstack(z1, vm[n])) for n in range(len(items))]
    zc = [hstack(cut(atm_b[h % 2], j, h), akv[n].astype(BF16)) for n, (j, h) in enumerate(items)]
    w = [_dot(xb[n], vstack(z2, zc[n])) for n in range(len(items))]
    tu = [w[n] + _dot(xb[n], vstack(w[n].astype(BF16), z2)) for n in range(len(items))]
    ry = [_dot(gb[n][c:], vstack(tu[n].astype(BF16), hstack(z1, vm[n]))) for n in range(len(items))]

    reff, yv, pm, qm = {}, {}, {}, {}
    for j in range(nch):
        for p in range(PAIRS):
            n0 = j * RWKV_HEADS + 2 * p
            tu_p = tu[n0] + tu[n0 + 1]
            ry_p = ry[n0] + ry[n0 + 1]
            h = 2 * p
            reff[j, p] = (cut(rt, j, h) + ry_p[:, :PAIR_W]).astype(BF16)
            yv[j, p] = ry_p[:, PAIR_W:]
            pm[j, p] = jnp.where(blk, _dot_tn(tu_p[:, :PAIR_W].astype(BF16), cut(btc_b, j, h)), 0.0).astype(BF16)
            qm[j, p] = jnp.where(blk, _dot_tn(vstack(tu_p[:, PAIR_W:].astype(BF16), cut(v_b, j, h)),
                                              vstack(cut(btc_b, j, h), cut(ktc_b, j, h))), 0.0)

    st = [state_ref[p] for p in range(PAIRS)]
    ys = []
    for j in range(nch):
        sb = [s.astype(BF16) for s in st]
        ys.append(jnp.concatenate([_dot_nt(reff[j, p], sb[p]) + yv[j, p] for p in range(PAIRS)], axis=1))
        st = [st[p] * gc_rows[j][:, p * PAIR_W:(p + 1) * PAIR_W] + _dot(sb[p], pm[j, p]) + qm[j, p]
              for p in range(PAIRS)]
    for p in range(PAIRS):
        state_ref[p] = st[p]
    sout_ref[0] = state_ref[...]

    y = ys[0] if nch == 1 else jnp.concatenate(ys, axis=0)
    mean = _seg_reduce(y, m64mean_ref[...])
    d = y - mean
    var = _seg_reduce(d * d, m64mean_ref[...])
    yn = d * lax.rsqrt(var + LNX_EPS) * lng_ref[...] + lnb_ref[...]
    bonus = _seg_reduce(r * k * rk_ref[...], m64sum_ref[...]) * v
    gate = _dot(jax.nn.sigmoid(gl_ref[0]).astype(BF16), g2_ref[...])
    rw_ref[0] = ((yn + bonus) * gate).astype(rw_ref.dtype)


def _state_to_pairs(s):
    z = jnp.zeros_like(s[:, 0::2])
    top = jnp.concatenate([s[:, 0::2], z], axis=-1)
    bot = jnp.concatenate([z, s[:, 1::2]], axis=-1)
    return jnp.concatenate([top, bot], axis=2)


def _pairs_to_state(sp):
    b = sp.shape[0]
    s = jnp.stack([sp[:, :, :RWKV_HEAD, :RWKV_HEAD], sp[:, :, RWKV_HEAD:, RWKV_HEAD:]], axis=2)
    return s.reshape(b, RWKV_HEADS, RWKV_HEAD, RWKV_HEAD)


def _wkv(seqs, s0, consts, tt):
    b, t, _ = seqs[0].shape
    nb = s0.shape[0]
    tok = lambda w: pl.BlockSpec((1, tt, w), lambda bi, i: (bi, i, 0))
    st_shape = (1, PAIRS, PAIR_W, PAIR_W)
    in_specs = [tok(RWKV_DIM)] * 6 + [tok(GATE_LORA)]
    in_specs.append(pl.BlockSpec(st_shape, (lambda bi, i: (bi, 0, 0, 0)) if nb > 1 else (lambda bi, i: (0, 0, 0, 0))))
    tri = jnp.kron(jnp.eye(tt // CHUNK, dtype=F32), jnp.tril(jnp.ones((CHUNK, CHUNK), F32))).astype(BF16)
    consts = (tri,) + tuple(consts)
    in_specs += [_full_spec(w) for w in consts]
    rw, sp = pl.pallas_call(
        _wkv_kernel, grid=(b, t // tt),
        in_specs=in_specs,
        out_specs=[tok(RWKV_DIM), pl.BlockSpec(st_shape, lambda bi, i: (bi, 0, 0, 0))],
        out_shape=[jax.ShapeDtypeStruct((b, t, RWKV_DIM), BF16),
                   jax.ShapeDtypeStruct((b,) + st_shape[1:], F32)],
        scratch_shapes=[pltpu.VMEM(st_shape[1:], F32)],
        compiler_params=_params(("arbitrary", "arbitrary")), name="wkv",
    )(*seqs, _state_to_pairs(s0), *consts)
    return rw, _pairs_to_state(sp)


def _pad_chunk(seqs):
    t = seqs[0].shape[1]
    pad = (-t) % CHUNK
    return [jnp.pad(a, ((0, 0), (0, pad), (0, 0))) for a in seqs]


NPAIR = MLA_HEADS // 2


def _attn_step(qh, kt, vt, mask, first, m_ref, l_ref, acc_ref, extra=None):
    n = vt(0).shape[0]
    tq = acc_ref.shape[1]
    even = lax.broadcasted_iota(jnp.int32, (n, LANE), 1) < V_DIM
    lane_q = lax.broadcasted_iota(jnp.int32, (tq, LANE), 1) < V_DIM
    s = [_dot_nt(qh(h), kt(h)) for h in range(MLA_HEADS)]
    if extra is not None:
        s = [s[h] + extra(h) for h in range(MLA_HEADS)]
    if mask is not None:
        s = [jnp.where(mask, x, NEG) for x in s]
    ps, alphas = [], []
    for h in range(MLA_HEADS):
        mx = jnp.max(s[h], axis=-1, keepdims=True)
        if first:
            m_new = mx
        else:
            m_old = m_ref[h]
            m_new = jnp.maximum(m_old, mx)
            alphas.append(jnp.exp(m_old - m_new))
        pr = jnp.exp(s[h] - m_new)
        sm = jnp.sum(pr, axis=-1, keepdims=True)
        l_ref[h] = sm if first else alphas[h] * l_ref[h] + sm
        m_ref[h] = m_new
        ps.append(pr.astype(BF16))
    for p in range(NPAIR):
        v = vt(p)
        zero = jnp.zeros_like(v)
        vcat = jnp.concatenate([jnp.where(even, v, zero), jnp.where(even, zero, v)], axis=0)
        pv = _dot(jnp.concatenate([ps[2 * p], ps[2 * p + 1]], axis=1), vcat)
        if first:
            acc_ref[p] = pv
        else:
            acc_ref[p] = jnp.where(lane_q, alphas[2 * p], alphas[2 * p + 1]) * acc_ref[p] + pv


def _attn_finish(l_ref, acc_ref, g):
    tq = acc_ref.shape[1]
    lane_q = lax.broadcasted_iota(jnp.int32, (tq, LANE), 1) < V_DIM
    o = jnp.concatenate([acc_ref[p] / jnp.where(lane_q, l_ref[2 * p], l_ref[2 * p + 1])
                         for p in range(NPAIR)], axis=1)
    return o * lax.rsqrt(jnp.mean(o * o, axis=-1, keepdims=True) + EPS) * g


def _attn_scratch(tq):
    return [pltpu.VMEM((MLA_HEADS, tq, 1), F32), pltpu.VMEM((MLA_HEADS, tq, 1), F32),
            pltpu.VMEM((NPAIR, tq, LANE), F32)]


_hq = lambda h: slice(h * HEAD_GROUP, (h + 1) * HEAD_GROUP)
_hv = lambda p: slice(p * LANE, (p + 1) * LANE)


L_ROWS = 16
ACC_ROWS = V_DIM + L_ROWS


def _attn_step_t(qh, kt, vt, mask, first, m_ref, acc_ref):
    n = vt(0).shape[1]
    ones = jnp.ones((L_ROWS, n), BF16)
    s = [_dot_nt(kt(h), qh(h)) for h in range(MLA_HEADS)]
    if mask is not None:
        s = [jnp.where(mask, x, NEG) for x in s]
    ps, alphas = [], []
    for h in range(MLA_HEADS):
        mx = jnp.max(s[h], axis=0, keepdims=True)
        if first:
            m_new = mx
        else:
            m_old = m_ref[h]
            m_new = jnp.maximum(m_old, mx)
            alphas.append(jnp.exp(m_old - m_new))
        m_ref[h] = m_new
        ps.append(jnp.exp(s[h] - m_new).astype(BF16))
    for h in range(MLA_HEADS):
        pv = _dot(jnp.concatenate([vt(h), ones], axis=0), ps[h])
        acc_ref[h] = pv if first else alphas[h] * acc_ref[h] + pv


def _attn_prompt_kernel(q_ref, k_ref, vt_ref, km_ref, vtm_ref, g_ref, o_ref, m_ref, acc_ref, *, tk):
    i = pl.program_id(1)
    tq = q_ref.shape[1]
    assert tq == tk
    st = (m_ref, acc_ref)
    qh = lambda h: q_ref[0, :, _hq(h)]
    hd = lambda h: slice(h * V_DIM, (h + 1) * V_DIM)

    meta_mask = lax.broadcasted_iota(jnp.int32, (LANE, tq), 0) < N_META
    _attn_step_t(qh, lambda h: km_ref[:, _hq(h)], lambda h: vtm_ref[hd(h), :], meta_mask, True, *st)

    def body(t, carry):
        ks = pl.ds(pl.multiple_of(t * tk, tk), tk)
        _attn_step_t(qh, lambda h: k_ref[0, ks, _hq(h)], lambda h: vt_ref[0, hd(h), ks], None, False, *st)
        return carry

    lax.fori_loop(0, i, body, 0)
    krow = lax.broadcasted_iota(jnp.int32, (tk, tq), 0) // CHUNK
    qcol = lax.broadcasted_iota(jnp.int32, (tk, tq), 1) // CHUNK
    ks = pl.ds(pl.multiple_of(i * tk, tk), tk)
    _attn_step_t(qh, lambda h: k_ref[0, ks, _hq(h)], lambda h: vt_ref[0, hd(h), ks], krow <= qcol, False, *st)

    o_t = jnp.concatenate([acc_ref[h, :V_DIM, :] / acc_ref[h, V_DIM:V_DIM + 1, :] for h in range(MLA_HEADS)],
                          axis=0)
    o_t = o_t * lax.rsqrt(jnp.mean(o_t * o_t, axis=0, keepdims=True) + EPS)
    o_ref[0] = (o_t.T * g_ref[...]).astype(o_ref.dtype)


def _attn_prompt(q, kc, vt, kmeta, vtmeta, g, tq):
    b, s, _ = q.shape
    return pl.pallas_call(
        functools.partial(_attn_prompt_kernel, tk=tq), grid=(b, s // tq),
        in_specs=[pl.BlockSpec((1, tq, QK_COLS), lambda bi, i: (bi, i, 0)),
                  pl.BlockSpec((1, s, QK_COLS), lambda bi, i: (bi, 0, 0)),
                  pl.BlockSpec((1, MLA_DIM, s), lambda bi, i: (bi, 0, 0)),
                  _full_spec(kmeta), _full_spec(vtmeta), _full_spec(g)],
        out_specs=pl.BlockSpec((1, tq, MLA_DIM), lambda bi, i: (bi, i, 0)),
        out_shape=jax.ShapeDtypeStruct((b, s, MLA_DIM), BF16),
        scratch_shapes=[pltpu.VMEM((MLA_HEADS, 1, tq), F32), pltpu.VMEM((MLA_HEADS, ACC_ROWS, tq), F32)],
        compiler_params=_params(("arbitrary", "arbitrary")), name="attn_prompt",
    )(q, kc, vt, kmeta, vtmeta, g)


def _attn_sample_kernel(q_ref, qr_ref, kn_ref, vn_ref, cc_ref, ck_ref, wkk_ref, gkn_ref, mk_ref,
                        wv_ref, g_ref, o_ref, m_ref, l_ref, acc_ref, *, tk, past):
    tq = q_ref.shape[1]
    st = (m_ref, l_ref, acc_ref)
    qh = lambda h: q_ref[0, :, _hq(h)]
    qchunk = (past + lax.broadcasted_iota(jnp.int32, (tq, 1), 0)) // CHUNK
    kchunk = (past + lax.broadcasted_iota(jnp.int32, (1, tq), 1)) // CHUNK
    _attn_step(qh, lambda h: kn_ref[0, :, _hq(h)], lambda p: vn_ref[0, :, _hv(p)], kchunk <= qchunk, True, *st)

    def body(t, carry):
        ks = pl.ds(pl.multiple_of(t * tk, tk), tk)
        cb = cc_ref[0, ks, :].astype(BF16)
        kraw = _dot(cb, wkk_ref[...])
        kn = kraw * lax.rsqrt(_seg_reduce(kraw * kraw, mk_ref[...], terms=1) + EPS) * gkn_ref[...]
        kc = kn.astype(BF16)
        vc = _dot(cb, wv_ref[...]).astype(BF16)
        srope = _dot_nt(qr_ref[0], ck_ref[0, ks, :].astype(BF16))
        kch = (t * tk + lax.broadcasted_iota(jnp.int32, (1, tk), 1)) // CHUNK
        _attn_step(qh, lambda h: kc[:, _hq(h)], lambda p: vc[:, _hv(p)], kch <= qchunk, False, *st,
                   extra=lambda h: srope[h * tq:(h + 1) * tq])
        return carry

    lax.fori_loop(0, past // tk, body, 0)
    o_ref[0] = _attn_finish(l_ref, acc_ref, g_ref[...]).astype(o_ref.dtype)


def _attn_sample(q, kn, vn, cache_c, cache_kr, wts, g, tk):
    b, t, _ = q.shape
    past = cache_c.shape[1]
    bspec = lambda shp: pl.BlockSpec((1,) + shp, lambda bi: (bi, 0, 0))
    qr = q.reshape(b, t, MLA_HEADS, HEAD_GROUP)[..., QK_NOPE:QK_NOPE + QK_ROPE]
    qr = jnp.swapaxes(qr, 1, 2).reshape(b, MLA_HEADS * t, QK_ROPE)
    return pl.pallas_call(
        functools.partial(_attn_sample_kernel, tk=tk, past=past), grid=(b,),
        in_specs=[bspec((t, QK_COLS)), bspec((MLA_HEADS * t, QK_ROPE)), bspec((t, QK_COLS)), bspec((t, MLA_DIM)),
                  bspec((past, KV_LORA)), bspec((past, QK_ROPE))]
                 + [_full_spec(w) for w in wts] + [_full_spec(g)],
        out_specs=bspec((t, MLA_DIM)),
        out_shape=jax.ShapeDtypeStruct((b, t, MLA_DIM), BF16),
        scratch_shapes=_attn_scratch(t),
        compiler_params=_params(("arbitrary",)), name="attn_sample",
    )(q, qr, kn, vn, cache_c, cache_kr, *wts, g)


def _tail_kernel(x_ref, at_ref, rw_ref, wo_ref, gf_ref, wg_ref, wu_ref, wd_ref, o_ref, *, fc):
    mix = jnp.concatenate([at_ref[...], rw_ref[...]], axis=1)
    h = x_ref[...] + _dot(mix, wo_ref[...])
    u = (h * lax.rsqrt(jnp.mean(h * h, axis=-1, keepdims=True) + EPS) * gf_ref[...]).astype(BF16)
    acc = h
    for j in range(D_FF // fc):
        cs = slice(j * fc, (j + 1) * fc)
        gt = _dot(u, wg_ref[:, cs])
        up = _dot(u, wu_ref[:, cs])
        act = (gt * jax.nn.sigmoid(gt) * up).astype(BF16)
        acc = acc + _dot(act, wd_ref[cs, :])
    o_ref[...] = acc


def _tail(x, attn, rw, wts, ts):
    n = x.shape[0]
    tok = lambda w: pl.BlockSpec((ts, w), lambda i: (i, 0))
    wspec = lambda w: pl.BlockSpec(w.shape, lambda i: (0,) * w.ndim, pipeline_mode=pl.Buffered(1))
    return pl.pallas_call(
        functools.partial(_tail_kernel, fc=MXU_N), grid=(n // ts,),
        in_specs=[tok(D_MODEL), tok(MLA_DIM), tok(RWKV_DIM)] + [wspec(w) for w in wts],
        out_specs=tok(D_MODEL),
        out_shape=jax.ShapeDtypeStruct((n, D_MODEL), F32),
        compiler_params=_params(("arbitrary",)), name="tail",
    )(x, attn, rw, *wts)


def _block_matrix(size, seg_of, scale_of):
    seg = np.array([seg_of(i) for i in range(size)])
    m = np.zeros((size, size), np.float32)
    for i in range(size):
        if seg[i] >= 0:
            m[i, seg == seg[i]] = scale_of(i)
    return jnp.asarray(m, BF16)


def _rope_tables(pos):
    half = QK_ROPE // 2
    inv = ROPE_BASE ** (-jnp.arange(half, dtype=F32) / half)
    ang = pos.astype(F32)[:, None] * inv[None, :]
    cos, sin = jnp.cos(ang), jnp.sin(ang)
    t = pos.shape[0]
    cc = jnp.concatenate([cos, cos], axis=1)
    ss = jnp.concatenate([-sin, sin], axis=1)
    zq = jnp.zeros((t, LANE - QK_NOPE - QK_ROPE), F32)
    cosq = jnp.concatenate([jnp.ones((t, QK_NOPE), F32), cc, zq], axis=1)
    sinq = jnp.concatenate([jnp.zeros((t, QK_NOPE), F32), ss, zq], axis=1)
    zk = jnp.zeros((t, LANE - QK_ROPE), F32)
    return cosq, sinq, jnp.concatenate([cc, zk], axis=1), jnp.concatenate([ss, zk], axis=1)


def _prep_weights(norm_mix_g, w_in, q_norm_g, w_q_up, kv_norm_g, w_kv_up, qn_nope_g, qn_rope_g,
                  kn_nope_g, kn_rope_g, mu_shift, w0, w2, a0, a2, k_k, k_a):
    half = QK_ROPE // 2
    swap = np.concatenate([np.arange(half, QK_ROPE), np.arange(half)])
    row = lambda v: v.astype(F32).reshape(1, -1)
    pad_to = lambda v, n: jnp.concatenate([v, jnp.zeros(v.shape[:-1] + (n - v.shape[-1],), v.dtype)], axis=-1)

    mla_cols = Q_LORA + KV_LORA + QK_ROPE
    w_kr = w_in[:, Q_LORA + KV_LORA:mla_cols]
    win = jnp.concatenate([pad_to(jnp.concatenate([w_in[:, :mla_cols], w_kr[:, swap]], axis=1), MLA_EXT),
                           w_in[:, mla_cols:]], axis=1).astype(BF16)

    qh = w_q_up.reshape(Q_LORA, MLA_HEADS, QK_NOPE + QK_ROPE)
    zq = jnp.zeros((Q_LORA, MLA_HEADS, QK_NOPE), w_q_up.dtype)
    wqa = pad_to(qh, HEAD_GROUP).reshape(Q_LORA, QK_COLS).astype(BF16)
    wqb = pad_to(jnp.concatenate([zq, qh[:, :, QK_NOPE:][:, :, swap]], axis=2), HEAD_GROUP)
    wqb = wqb.reshape(Q_LORA, QK_COLS).astype(BF16)
    gqa = jnp.tile(pad_to(jnp.concatenate([qn_nope_g, qn_rope_g]), HEAD_GROUP), MLA_HEADS)
    gqb = jnp.tile(pad_to(jnp.concatenate([jnp.zeros_like(qn_nope_g), qn_rope_g[swap]]), HEAD_GROUP), MLA_HEADS)

    def seg_q(i):
        j = i % HEAD_GROUP
        base = (i // HEAD_GROUP) * 2
        return base if j < QK_NOPE else (base + 1 if j < QK_NOPE + QK_ROPE else -1)
    mq = _block_matrix(MXU_N, seg_q, lambda i: 1.0 / (QK_NOPE if i % HEAD_GROUP < QK_NOPE else QK_ROPE))
    mk = _block_matrix(MXU_N, lambda i: i // HEAD_GROUP if i % HEAD_GROUP < QK_NOPE else -1,
                       lambda i: 1.0 / QK_NOPE)
    m64sum = _block_matrix(MXU_N, lambda i: i // RWKV_HEAD, lambda i: 1.0)
    m64mean = _block_matrix(MXU_N, lambda i: i // RWKV_HEAD, lambda i: 1.0 / RWKV_HEAD)

    kvh = w_kv_up.reshape(KV_LORA, MLA_HEADS, QK_NOPE + V_DIM)
    wkk = pad_to(kvh[:, :, :QK_NOPE], HEAD_GROUP).reshape(KV_LORA, QK_COLS).astype(BF16)
    wv = kvh[:, :, QK_NOPE:].reshape(KV_LORA, MLA_DIM).astype(BF16)
    gkn = jnp.tile(pad_to(kn_nope_g, HEAD_GROUP), MLA_HEADS)
    erep_np = np.zeros((LANE, QK_COLS), np.float32)
    for h in range(MLA_HEADS):
        for j in range(QK_ROPE):
            erep_np[j, h * HEAD_GROUP + QK_NOPE + j] = 1.0
    erep = jnp.asarray(erep_np, BF16)
    gkr = pad_to(kn_rope_g, LANE)
    gkrs = pad_to(kn_rope_g[swap], LANE)

    w2e = jnp.concatenate([w2, jnp.zeros((AAA_LORA, RWKV_DIM), w2.dtype)], axis=0).astype(BF16)
    a2e = jnp.concatenate([jnp.zeros((DECAY_LORA, RWKV_DIM), a2.dtype), a2], axis=0).astype(BF16)

    inproj_w = [row(norm_mix_g), win, row(q_norm_g), wqa, wqb, row(gqa), row(gqb), mq,
                row(kv_norm_g), row(gkr), row(gkrs), wkk, row(gkn), mk, erep, wv, wv.T,
                row(mu_shift), row(w0), w2e, row(a0), a2e, row(k_k), row(k_a), m64sum]
    sample_w = [wkk, row(gkn), mk, wv]
    return inproj_w, sample_w, m64mean, m64sum


def kernel(x_prompt, x_sample, cache_kv_latent, cache_k_rope, state_wkv, state_shift, meta_tokens, norm_mix_g, w_in, q_norm_g, w_q_up, kv_norm_g, w_kv_up, qn_nope_g, qn_rope_g, kn_nope_g, kn_rope_g, attn_out_g, mu_shift, w0, w2, a0, a2, g2, k_k, k_a, r_k, lnx_g, lnx_b, w_out, norm_ffn_g, w_gate, w_up, w_down):
    b, s, _ = x_prompt.shape
    db, ds, _ = x_sample.shape
    past = cache_kv_latent.shape[1]
    row = lambda v: v.astype(F32).reshape(1, -1)

    inproj_w, sample_w, m64mean, m64sum = _prep_weights(
        norm_mix_g, w_in, q_norm_g, w_q_up, kv_norm_g, w_kv_up, qn_nope_g, qn_rope_g,
        kn_nope_g, kn_rope_g, mu_shift, w0, w2, a0, a2, k_k, k_a)
    wkv_consts = (g2.astype(BF16), row(r_k), row(lnx_g), row(lnx_b), m64mean, m64sum)
    tail_w = [w_out.astype(BF16), row(norm_ffn_g), w_gate.astype(BF16), w_up.astype(BF16), w_down.astype(BF16)]
    g_attn = row(attn_out_g)

    ts = min(512, s)
    meta = _inproj(meta_tokens.astype(F32)[None], jnp.zeros((1, 1, RWKV_COLS), F32),
                   _rope_tables(jnp.arange(N_META)), inproj_w, N_META, values_transposed=True)
    c_m, kr_m, _, kc_m, vt_m = meta[:5]
    _, s_meta = _wkv(_pad_chunk(meta[5:12]), jnp.zeros((1, RWKV_HEADS, RWKV_HEAD, RWKV_HEAD), F32),
                     wkv_consts, CHUNK)

    pr = _inproj(x_prompt, meta[12], _rope_tables(N_META + jnp.arange(s)), inproj_w, ts, values_transposed=True)
    c_p, kr_p, q_p, kc_p, vt_p = pr[:5]
    rw_p, wkv_p = _wkv(pr[5:12], s_meta, wkv_consts, min(4 * CHUNK, s))
    kc_m = jnp.pad(kc_m[0], ((0, LANE - N_META), (0, 0)))
    vt_m = jnp.pad(vt_m[0], ((0, 0), (0, LANE - N_META)))
    attn_p = _attn_prompt(q_p, kc_p, vt_p, kc_m, vt_m, g_attn, min(512, s))
    y_prompt = _tail(x_prompt.reshape(b * s, D_MODEL), attn_p.reshape(b * s, MLA_DIM),
                     rw_p.reshape(b * s, RWKV_DIM), tail_w, ts).reshape(b, s, D_MODEL)
    kv_latent_p = jnp.concatenate([jnp.broadcast_to(c_m, (b, N_META, KV_LORA)), c_p], axis=1)
    k_rope_p = jnp.concatenate([jnp.broadcast_to(kr_m, (b, N_META, QK_ROPE)), kr_p], axis=1)

    sm = _inproj(x_sample, state_shift.astype(F32)[:, None, :], _rope_tables(past + jnp.arange(ds)),
                 inproj_w, ds)
    c_s, kr_s, q_s, kc_s, vc_s = sm[:5]
    rw_s, wkv_s = _wkv(_pad_chunk(sm[5:12]), state_wkv.astype(F32), wkv_consts, CHUNK)
    rw_s = rw_s[:, :ds]
    attn_s = _attn_sample(q_s, kc_s, vc_s, cache_kv_latent, cache_k_rope, sample_w, g_attn, min(512, past))
    y_sample = _tail(x_sample.reshape(db * ds, D_MODEL), attn_s.reshape(db * ds, MLA_DIM),
                     rw_s.reshape(db * ds, RWKV_DIM), tail_w, min(512, db * ds)).reshape(db, ds, D_MODEL)

    return (y_prompt, y_sample, kv_latent_p, k_rope_p, wkv_p, pr[12][:, 0, :],
            c_s, kr_s, wkv_s, sm[12][:, 0, :])
```

```python
import functools

import numpy as np
import jax
import jax.numpy as jnp
from jax import lax
from jax.experimental import pallas as pl
from jax.experimental.pallas import tpu as pltpu

F32 = jnp.float32
BF16 = jnp.bfloat16

D_MODEL = 1024
CHUNK = 64
N_META = 16
EPS = 1e-6
NEG = -1e30
MLA_HEADS = 8
QK_NOPE = 64
QK_ROPE = 32
V_DIM = 64
Q_LORA = 256
KV_LORA = 128
ROPE_BASE = 10000.0
ATTN_SCALE = (QK_NOPE + QK_ROPE) ** -0.5
LOG2E = float(np.log2(np.e))
MLA_DIM = MLA_HEADS * V_DIM
RWKV_HEADS = 8
RWKV_HEAD = 64
RWKV_DIM = RWKV_HEADS * RWKV_HEAD
DECAY_LORA = 64
AAA_LORA = 64
GATE_LORA = 128
LNX_EPS = 64e-5
RWKV_COLS = 3 * RWKV_DIM + DECAY_LORA + AAA_LORA + GATE_LORA
D_FF = -(-8 * D_MODEL // (3 * 256)) * 256

LANE = 128
MXU_N = 256
HEAD_GROUP = LANE
QK_COLS = MLA_HEADS * HEAD_GROUP
MLA_EXT = 512
IN_EXT = MLA_EXT + RWKV_COLS
VMEM_LIMIT = 56 * 1024 * 1024


def _dot(a, b):
    return jnp.dot(a, b, preferred_element_type=F32)


def _dot_nt(a, b):
    return lax.dot_general(a, b, (((1,), (1,)), ((), ())), preferred_element_type=F32)


def _dot_tn(a, b):
    return lax.dot_general(a, b, (((0,), (0,)), ((), ())), preferred_element_type=F32)


def _seg_reduce(x, m, terms=2):
    parts = [x.astype(BF16)]
    if terms == 2:
        parts.append((x - parts[0].astype(F32)).astype(BF16))
    outs = []
    for j in range(x.shape[1] // MXU_N):
        sl = slice(j * MXU_N, (j + 1) * MXU_N)
        outs.append(sum(_dot(part[:, sl], m) for part in parts))
    return outs[0] if len(outs) == 1 else jnp.concatenate(outs, axis=1)


def _full_spec(arr):
    nd = arr.ndim
    return pl.BlockSpec(arr.shape, lambda *_: (0,) * nd)


def _params(sem):
    return pltpu.CompilerParams(dimension_semantics=sem, vmem_limit_bytes=VMEM_LIMIT)


def _inproj_kernel(x_ref, prev0_ref, cosq_ref, sinq_ref, cosk_ref, sink_ref,
                   gmix_ref, win_ref, gql_ref, wqa_ref, wqb_ref, gqa_ref, gqb_ref, mq_ref,
                   gkv_ref, gkr_ref, gkrs_ref, wkk_ref, gkn_ref, mk_ref, erep_ref, wv_ref, wvt_ref,
                   mu_ref, w0_ref, w2_ref, a0_ref, a2_ref, kkg_ref, ka_ref, m64_ref,
                   c_ref, kr_ref, q_ref, kc_ref, vc_ref,
                   r_ref, lw_ref, k_ref, v_ref, kk_ref, b_ref, gl_ref, shift_ref,
                   carry_ref, *, values_transposed):
    i = pl.program_id(1)
    ts = x_ref.shape[1]

    x = x_ref[0]
    xn = x * lax.rsqrt(jnp.mean(x * x, axis=-1, keepdims=True) + EPS) * gmix_ref[...]
    p = _dot(xn.astype(BF16), win_ref[...])

    pq = p[:, :Q_LORA]
    ql = pq * lax.rsqrt(jnp.mean(pq * pq, axis=-1, keepdims=True) + EPS) * gql_ref[...]
    ql = ql.astype(BF16)
    qa = _dot(ql, wqa_ref[...])
    qb = _dot(ql, wqb_ref[...])
    qs = lax.rsqrt(_seg_reduce(qa * qa, mq_ref[...], terms=1) + EPS)
    cosq = jnp.concatenate([cosq_ref[...]] * MLA_HEADS, axis=1)
    sinq = jnp.concatenate([sinq_ref[...]] * MLA_HEADS, axis=1)
    q = qs * (qa * gqa_ref[...] * cosq + qb * gqb_ref[...] * sinq) * (ATTN_SCALE * LOG2E)
    q_ref[0] = q.astype(BF16)

    pc = p[:, Q_LORA:Q_LORA + KV_LORA]
    c = pc * lax.rsqrt(jnp.mean(pc * pc, axis=-1, keepdims=True) + EPS) * gkv_ref[...]
    c_ref[0] = c
    cb = c.astype(BF16)

    pk = p[:, Q_LORA + KV_LORA:MLA_EXT]
    lane = lax.broadcasted_iota(jnp.int32, pk.shape, 1)
    ssk = jnp.sum(jnp.where(lane < QK_ROPE, pk * pk, 0.0), axis=-1, keepdims=True)
    sk = lax.rsqrt(ssk * (1.0 / QK_ROPE) + EPS)
    pk_sw = pltpu.roll(pk, LANE - QK_ROPE, axis=1)
    kr = sk * (pk * gkr_ref[...] * cosk_ref[...] + pk_sw * gkrs_ref[...] * sink_ref[...])
    kr_ref[0] = kr[:, :QK_ROPE]

    kraw = _dot(cb, wkk_ref[...])
    kn = kraw * lax.rsqrt(_seg_reduce(kraw * kraw, mk_ref[...], terms=1) + EPS) * gkn_ref[...]
    kc_ref[0] = (kn + _dot(kr.astype(BF16), erep_ref[...])).astype(BF16)
    if values_transposed:
        vc_ref[0] = _dot_nt(wvt_ref[...], cb).astype(BF16)
    else:
        vc_ref[0] = _dot(cb, wv_ref[...]).astype(BF16)

    prw = p[:, MLA_EXT:]

    @pl.when(i == 0)
    def _():
        carry_ref[...] = prev0_ref[0]

    row = lax.broadcasted_iota(jnp.int32, (ts, 1), 0)
    shifted = jnp.where(row == 0, carry_ref[...], pltpu.roll(prw, 1, axis=0))
    last = prw[ts - 1:ts, :]
    carry_ref[...] = last
    shift_ref[0] = last
    xm = prw + (shifted - prw) * mu_ref[...]

    o = RWKV_DIM
    r = xm[:, :o]
    k = xm[:, o:2 * o]
    v = xm[:, 2 * o:3 * o]
    wa = xm[:, 3 * o:3 * o + DECAY_LORA + AAA_LORA]
    gl_ref[0] = xm[:, 3 * o + DECAY_LORA + AAA_LORA:]
    dw = _dot(jnp.tanh(wa).astype(BF16), w2_ref[...])
    da = _dot(wa.astype(BF16), a2_ref[...])
    lw_ref[0] = -jax.nn.sigmoid(w0_ref[...] + dw) * float(np.exp(-0.5))
    a = jax.nn.sigmoid(a0_ref[...] + da)
    kx = k * kkg_ref[...]
    kk = kx * lax.rsqrt(jnp.maximum(_seg_reduce(kx * kx, m64_ref[...], terms=1), 1e-24))
    r_ref[0] = r
    k_ref[0] = k * (1.0 + (a - 1.0) * ka_ref[...])
    v_ref[0] = v
    kk_ref[0] = kk
    b_ref[0] = kk * a


def _inproj(x, prev0, tabs, wts, ts, values_transposed=False):
    b, t, _ = x.shape
    nb = prev0.shape[0]
    grid = (b, t // ts)
    tok = lambda w: pl.BlockSpec((1, ts, w), lambda bi, i: (bi, i, 0))
    in_specs = [tok(D_MODEL),
                pl.BlockSpec((1, 1, RWKV_COLS), (lambda bi, i: (bi, 0, 0)) if nb > 1 else (lambda bi, i: (0, 0, 0)))]
    in_specs += [pl.BlockSpec((ts, LANE), lambda bi, i: (i, 0)) for _ in tabs]
    in_specs += [_full_spec(w) for w in wts]
    widths = [(KV_LORA, F32), (QK_ROPE, F32), (QK_COLS, BF16), (QK_COLS, BF16), (MLA_DIM, BF16)]
    widths += [(RWKV_DIM, F32)] * 6 + [(GATE_LORA, F32)]
    out_shape = [jax.ShapeDtypeStruct((b, t, w), dt) for w, dt in widths]
    out_specs = [tok(w) for w, _ in widths]
    if values_transposed:
        out_shape[4] = jax.ShapeDtypeStruct((b, MLA_DIM, t), BF16)
        out_specs[4] = pl.BlockSpec((1, MLA_DIM, ts), lambda bi, i: (bi, 0, i))
    out_shape.append(jax.ShapeDtypeStruct((b, 1, RWKV_COLS), F32))
    out_specs.append(pl.BlockSpec((1, 1, RWKV_COLS), lambda bi, i: (bi, 0, 0)))
    return pl.pallas_call(
        functools.partial(_inproj_kernel, values_transposed=values_transposed), grid=grid, in_specs=in_specs, out_specs=out_specs, out_shape=out_shape,
        scratch_shapes=[pltpu.VMEM((1, RWKV_COLS), F32)],
        compiler_params=_params(("arbitrary", "arbitrary")), name="inproj",
    )(x, prev0, *tabs, *wts)


PAIRS = RWKV_HEADS // 2
PAIR_W = 2 * RWKV_HEAD


def _wkv_kernel(r_ref, lw_ref, k_ref, v_ref, kk_ref, b_ref, gl_ref, s0_ref,
                tri_ref, g2_ref, rk_ref, lng_ref, lnb_ref, m64mean_ref, m64sum_ref,
                rw_ref, sout_ref, state_ref):
    i = pl.program_id(1)
    tt = r_ref.shape[1]
    c = CHUNK
    nch = tt // c

    @pl.when(i == 0)
    def _():
        state_ref[...] = s0_ref[0]

    r = r_ref[0]
    lw = lw_ref[0]
    k = k_ref[0]
    v = v_ref[0]
    tri = tri_ref[...]
    l1 = lw.astype(BF16)
    e1 = lw - l1.astype(F32)
    l2 = e1.astype(BF16)
    l3 = (e1 - l2.astype(F32)).astype(BF16)
    cum = _dot(tri, l1) + _dot(tri, l2) + _dot(tri, l3)
    g = jnp.exp(cum)
    gi = jnp.exp(-cum)
    gp = jnp.exp(cum - lw)
    gc_rows = [g[j * c + c - 1:(j + 1) * c, :] for j in range(nch)]
    gcb = jnp.concatenate([jnp.broadcast_to(x, (c, RWKV_DIM)) for x in gc_rows], axis=0)
    at = -(kk_ref[0] * gp)
    bt = b_ref[0] * gi
    kt = k * gi
    rt = r * g

    even = (lax.broadcasted_iota(jnp.int32, (tt, RWKV_DIM), 1) % PAIR_W) < RWKV_HEAD
    split = lambda x: (jnp.where(even, x, 0.0).astype(BF16), jnp.where(even, 0.0, x).astype(BF16))
    atm_b, rtm_b, vm_b = split(at), split(rt), split(v)
    bt_b, kt_b, v_b = bt.astype(BF16), kt.astype(BF16), v.astype(BF16)
    btc_b, ktc_b = (bt * gcb).astype(BF16), (kt * gcb).astype(BF16)

    lane = lax.broadcasted_iota(jnp.int32, (c, PAIR_W), 1)
    lo = lane < c
    eye_hi = (lane == lax.broadcasted_iota(jnp.int32, (c, PAIR_W), 0) + c).astype(F32)
    rowi = lax.broadcasted_iota(jnp.int32, (2 * c, 2 * c), 0)
    coli = lax.broadcasted_iota(jnp.int32, (2 * c, 2 * c), 1)
    colt = jnp.where(coli >= c, coli - c, coli)
    gmask = colt < jnp.where(rowi < c, rowi, rowi - c + 1)
    blk = (rowi // c) == (coli // c)
    z1 = jnp.zeros((c, PAIR_W), BF16)
    z2 = jnp.zeros((c, 2 * PAIR_W), BF16)

    items = [(j, h) for j in range(nch) for h in range(RWKV_HEADS)]
    rows = lambda j: slice(j * c, (j + 1) * c)
    lanes = lambda h: slice((h // 2) * PAIR_W, (h // 2 + 1) * PAIR_W)
    cut = lambda x, j, h: x[rows(j), lanes(h)]
    vstack = lambda a, b: jnp.concatenate([a, b], axis=0)
    hstack = lambda a, b: jnp.concatenate([a, b], axis=1)

    gb, xs = [], []
    for j, h in items:
        ar = vstack(cut(atm_b[h % 2], j, h), cut(rtm_b[h % 2], j, h))
        bk = vstack(cut(bt_b, j, h), cut(kt_b, j, h))
        gm = jnp.where(gmask, _dot_nt(ar, bk), 0.0)
        gb.append(gm.astype(BF16))
        xs.append(jnp.where(lo, gm[:c], eye_hi))
    for _ in range(int(np.log2(c)) - 1):
        xs = [_dot(x[:, :c].astype(BF16), x.astype(BF16)) + jnp.where(lo, 0.0, x) for x in xs]
    xb = [x.astype(BF16) for x in xs]
    vm = [cut(vm_b[h % 2], j, h) for j, h in items]
    akv = [_dot(gb[n][:c], vstack(z1, vm[n])) for n in range(len(items))]
    zc = [hstack(cut(atm_b[h % 2], j, h), akv[n].astype(BF16)) for n, (j, h) in enumerate(items)]
    w = [_dot(xb[n], vstack(z2, zc[n])) for n in range(len(items))]
    tu = [w[n] + _dot(xb[n], vstack(w[n].astype(BF16), z2)) for n in range(len(items))]
    ry = [_dot(gb[n][c:], vstack(tu[n].astype(BF16), hstack(z1, vm[n]))) for n in range(len(items))]

    reff, yv, pm, qm = {}, {}, {}, {}
    for j in range(nch):
        for p in range(PAIRS):
            n0 = j * RWKV_HEADS + 2 * p
            tu_p = tu[n0] + tu[n0 + 1]
            ry_p = ry[n0] + ry[n0 + 1]
            h = 2 * p
            reff[j, p] = (cut(rt, j, h) + ry_p[:, :PAIR_W]).astype(BF16)
            yv[j, p] = ry_p[:, PAIR_W:]
            pm[j, p] = jnp.where(blk, _dot_tn(tu_p[:, :PAIR_W].astype(BF16), cut(btc_b, j, h)), 0.0).astype(BF16)
            qm[j, p] = jnp.where(blk, _dot_tn(vstack(tu_p[:, PAIR_W:].astype(BF16), cut(v_b, j, h)),
                                              vstack(cut(btc_b, j, h), cut(ktc_b, j, h))), 0.0)

    st = [state_ref[p] for p in range(PAIRS)]
    ys = []
    for j in range(nch):
        sb = [s.astype(BF16) for s in st]
        ys.append(jnp.concatenate([_dot_nt(reff[j, p], sb[p]) + yv[j, p] for p in range(PAIRS)], axis=1))
        st = [st[p] * gc_rows[j][:, p * PAIR_W:(p + 1) * PAIR_W] + _dot(sb[p], pm[j, p]) + qm[j, p]
              for p in range(PAIRS)]
    for p in range(PAIRS):
        state_ref[p] = st[p]
    sout_ref[0] = state_ref[...]

    y = ys[0] if nch == 1 else jnp.concatenate(ys, axis=0)
    mean = _seg_reduce(y, m64mean_ref[...])
    d = y - mean
    var = _seg_reduce(d * d, m64mean_ref[...], terms=1)
    yn = d * lax.rsqrt(var + LNX_EPS) * lng_ref[...] + lnb_ref[...]
    bonus = _seg_reduce(r * k * rk_ref[...], m64sum_ref[...], terms=1) * v
    gate = _dot(jax.nn.sigmoid(gl_ref[0]).astype(BF16), g2_ref[...])
    rw_ref[0] = ((yn + bonus) * gate).astype(rw_ref.dtype)


def _state_to_pairs(s):
    z = jnp.zeros_like(s[:, 0::2])
    top = jnp.concatenate([s[:, 0::2], z], axis=-1)
    bot = jnp.concatenate([z, s[:, 1::2]], axis=-1)
    return jnp.concatenate([top, bot], axis=2)


def _pairs_to_state(sp):
    b = sp.shape[0]
    s = jnp.stack([sp[:, :, :RWKV_HEAD, :RWKV_HEAD], sp[:, :, RWKV_HEAD:, RWKV_HEAD:]], axis=2)
    return s.reshape(b, RWKV_HEADS, RWKV_HEAD, RWKV_HEAD)


def _wkv(seqs, s0, consts, tt):
    b, t, _ = seqs[0].shape
    nb = s0.shape[0]
    tok = lambda w: pl.BlockSpec((1, tt, w), lambda bi, i: (bi, i, 0))
    st_shape = (1, PAIRS, PAIR_W, PAIR_W)
    in_specs = [tok(RWKV_DIM)] * 6 + [tok(GATE_LORA)]
    in_specs.append(pl.BlockSpec(st_shape, (lambda bi, i: (bi, 0, 0, 0)) if nb > 1 else (lambda bi, i: (0, 0, 0, 0))))
    tri = jnp.kron(jnp.eye(tt // CHUNK, dtype=F32), jnp.tril(jnp.ones((CHUNK, CHUNK), F32))).astype(BF16)
    consts = (tri,) + tuple(consts)
    in_specs += [_full_spec(w) for w in consts]
    rw, sp = pl.pallas_call(
        _wkv_kernel, grid=(b, t // tt),
        in_specs=in_specs,
        out_specs=[tok(RWKV_DIM), pl.BlockSpec(st_shape, lambda bi, i: (bi, 0, 0, 0))],
        out_shape=[jax.ShapeDtypeStruct((b, t, RWKV_DIM), BF16),
                   jax.ShapeDtypeStruct((b,) + st_shape[1:], F32)],
        scratch_shapes=[pltpu.VMEM(st_shape[1:], F32)],
        compiler_params=_params(("arbitrary", "arbitrary")), name="wkv",
    )(*seqs, _state_to_pairs(s0), *consts)
    return rw, _pairs_to_state(sp)


def _pad_chunk(seqs):
    t = seqs[0].shape[1]
    pad = (-t) % CHUNK
    return [jnp.pad(a, ((0, 0), (0, pad), (0, 0))) for a in seqs]


_hq = lambda h: slice(h * HEAD_GROUP, (h + 1) * HEAD_GROUP)


L_ROWS = 16
ACC_ROWS = V_DIM + L_ROWS


def _attn_step_t(qh, kt, vt, mask, first, m_ref, acc_ref, qw=slice(None)):
    n = vt(0).shape[1]
    ones = jnp.ones((L_ROWS, n), BF16)
    s = [_dot_nt(kt(h), qh(h)) for h in range(MLA_HEADS)]
    if mask is not None:
        s = [jnp.where(mask, x, NEG) for x in s]
    ps, alphas = [], []
    for h in range(MLA_HEADS):
        mx = jnp.max(s[h], axis=0, keepdims=True)
        if first:
            m_new = mx
        else:
            m_old = m_ref[h, :, qw]
            m_new = jnp.maximum(m_old, mx)
            alphas.append(jnp.exp2(m_old - m_new))
        m_ref[h, :, qw] = m_new
        ps.append(jnp.exp2(s[h] - m_new).astype(BF16))
    for h in range(MLA_HEADS):
        pv = _dot(jnp.concatenate([vt(h), ones], axis=0), ps[h])
        acc_ref[h, :, qw] = pv if first else alphas[h] * acc_ref[h, :, qw] + pv


def _attn_prompt_kernel(q_ref, k_ref, vt_ref, km_ref, vtm_ref, g_ref, o_ref, m_ref, acc_ref, *, tk):
    i = pl.program_id(1)
    tq = q_ref.shape[1]
    assert tq == tk and tq % (2 * CHUNK) == 0
    half = tq // 2
    st = (m_ref, acc_ref)
    qh = lambda h: q_ref[0, :, _hq(h)]
    hd = lambda h: slice(h * V_DIM, (h + 1) * V_DIM)

    _attn_step_t(qh, lambda h: km_ref[:, _hq(h)], lambda h: vtm_ref[hd(h), :], None, True, *st)

    def body(t, carry):
        ks = pl.ds(pl.multiple_of(t * tk, tk), tk)
        _attn_step_t(qh, lambda h: k_ref[0, ks, _hq(h)], lambda h: vt_ref[0, hd(h), ks], None, False, *st)
        return carry

    lax.fori_loop(0, i, body, 0)
    k0 = pl.ds(pl.multiple_of(i * tk, tk), half)
    k1 = pl.ds(pl.multiple_of(i * tk + half, half), half)
    krow = lax.broadcasted_iota(jnp.int32, (half, tq), 0) // CHUNK
    qcol = lax.broadcasted_iota(jnp.int32, (half, tq), 1) // CHUNK
    _attn_step_t(qh, lambda h: k_ref[0, k0, _hq(h)], lambda h: vt_ref[0, hd(h), k0], krow <= qcol, False, *st)
    _attn_step_t(lambda h: q_ref[0, half:, _hq(h)], lambda h: k_ref[0, k1, _hq(h)],
                 lambda h: vt_ref[0, hd(h), k1], (krow <= qcol)[:, :half], False, *st, qw=slice(half, tq))

    o_t = jnp.concatenate([acc_ref[h, :V_DIM, :] / acc_ref[h, V_DIM:V_DIM + 1, :] for h in range(MLA_HEADS)],
                          axis=0)
    o_t = o_t * lax.rsqrt(jnp.mean(o_t * o_t, axis=0, keepdims=True) + EPS)
    o_ref[0] = (o_t.T * g_ref[...]).astype(o_ref.dtype)


def _attn_prompt(q, kc, vt, kmeta, vtmeta, g, tq):
    b, s, _ = q.shape
    return pl.pallas_call(
        functools.partial(_attn_prompt_kernel, tk=tq), grid=(b, s // tq),
        in_specs=[pl.BlockSpec((1, tq, QK_COLS), lambda bi, i: (bi, i, 0)),
                  pl.BlockSpec((1, s, QK_COLS), lambda bi, i: (bi, 0, 0)),
                  pl.BlockSpec((1, MLA_DIM, s), lambda bi, i: (bi, 0, 0)),
                  _full_spec(kmeta), _full_spec(vtmeta), _full_spec(g)],
        out_specs=pl.BlockSpec((1, tq, MLA_DIM), lambda bi, i: (bi, i, 0)),
        out_shape=jax.ShapeDtypeStruct((b, s, MLA_DIM), BF16),
        scratch_shapes=[pltpu.VMEM((MLA_HEADS, 1, tq), F32), pltpu.VMEM((MLA_HEADS, ACC_ROWS, tq), F32)],
        compiler_params=_params(("arbitrary", "arbitrary")), name="attn_prompt",
    )(q, kc, vt, kmeta, vtmeta, g)


def _attn_sample_kernel(q_ref, qr_ref, cn_ref, krn_ref, cc_ref, ck_ref, wkc_ref, gk_ref, mct_ref, wv_ref,
                        g_ref, o_ref, m_ref, l_ref, acc_ref, *, tk, past):
    tq = q_ref.shape[1]
    rows = MLA_HEADS * tq
    qp = jnp.concatenate(
        [_dot_nt((q_ref[0, :, h * HEAD_GROUP:h * HEAD_GROUP + QK_NOPE].astype(F32) * gk_ref[...]).astype(BF16),
                 wkc_ref[:, h * QK_NOPE:(h + 1) * QK_NOPE]) for h in range(MLA_HEADS)], axis=0).astype(BF16)
    qr = qr_ref[0]
    qchunk = (past + lax.broadcasted_iota(jnp.int32, (rows, 1), 0) % tq) // CHUNK

    def step(cb, krb, kpos, first):
        n = cb.shape[0]
        kraw = _dot(cb, wkc_ref[...])
        ss_t = _dot_nt(mct_ref[...], (kraw * kraw).astype(BF16))
        sc_t = lax.rsqrt(ss_t + EPS)
        scale = jnp.concatenate([jnp.broadcast_to(sc_t[h:h + 1, :], (tq, n)) for h in range(MLA_HEADS)], axis=0)
        s = _dot_nt(qp, cb) * scale + _dot_nt(qr, krb)
        kchunk = (kpos + lax.broadcasted_iota(jnp.int32, (1, n), 1)) // CHUNK
        s = jnp.where(kchunk <= qchunk, s, NEG)
        mx = jnp.max(s, axis=-1, keepdims=True)
        if first:
            m_new = mx
        else:
            m_old = m_ref[...]
            m_new = jnp.maximum(m_old, mx)
            alpha = jnp.exp2(m_old - m_new)
        pr = jnp.exp2(s - m_new)
        sm = jnp.sum(pr, axis=-1, keepdims=True)
        pc = _dot(pr.astype(BF16), cb)
        m_ref[...] = m_new
        l_ref[...] = sm if first else alpha * l_ref[...] + sm
        acc_ref[...] = pc if first else alpha * acc_ref[...] + pc

    step(cn_ref[0].astype(BF16), krn_ref[0].astype(BF16), past, True)

    def body(t, carry):
        ks = pl.ds(pl.multiple_of(t * tk, tk), tk)
        step(cc_ref[0, ks, :].astype(BF16), ck_ref[0, ks, :].astype(BF16), t * tk, False)
        return carry

    lax.fori_loop(0, past // tk, body, 0)
    oc = (acc_ref[...] / l_ref[...]).astype(BF16)
    oall = _dot(oc, wv_ref[...])
    lane_head = lax.broadcasted_iota(jnp.int32, (tq, MLA_DIM), 1) // V_DIM
    o = jnp.zeros((tq, MLA_DIM), F32)
    for h in range(MLA_HEADS):
        o = jnp.where(lane_head == h, oall[h * tq:(h + 1) * tq], o)
    o = o * lax.rsqrt(jnp.mean(o * o, axis=-1, keepdims=True) + EPS) * g_ref[...]
    o_ref[0] = o.astype(o_ref.dtype)


def _attn_sample(q, c_new, kr_new, cache_c, cache_kr, wts, g, tk):
    b, t, _ = q.shape
    past = cache_c.shape[1]
    rows = MLA_HEADS * t
    bspec = lambda shp: pl.BlockSpec((1,) + shp, lambda bi: (bi, 0, 0))
    qr = q.reshape(b, t, MLA_HEADS, HEAD_GROUP)[..., QK_NOPE:QK_NOPE + QK_ROPE]
    qr = jnp.swapaxes(qr, 1, 2).reshape(b, rows, QK_ROPE)
    return pl.pallas_call(
        functools.partial(_attn_sample_kernel, tk=tk, past=past), grid=(b,),
        in_specs=[bspec((t, QK_COLS)), bspec((rows, QK_ROPE)), bspec((t, KV_LORA)), bspec((t, QK_ROPE)),
                  bspec((past, KV_LORA)), bspec((past, QK_ROPE))]
                 + [_full_spec(w) for w in wts] + [_full_spec(g)],
        out_specs=bspec((t, MLA_DIM)),
        out_shape=jax.ShapeDtypeStruct((b, t, MLA_DIM), BF16),
        scratch_shapes=[pltpu.VMEM((rows, 1), F32), pltpu.VMEM((rows, 1), F32), pltpu.VMEM((rows, KV_LORA), F32)],
        compiler_params=_params(("arbitrary",)), name="attn_sample",
    )(q, qr, c_new, kr_new, cache_c, cache_kr, *wts, g)


def _tail_kernel(x_ref, at_ref, rw_ref, wo_ref, gf_ref, wg_ref, wu_ref, wd_ref, o_ref, *, fc):
    mix = jnp.concatenate([at_ref[...], rw_ref[...]], axis=1)
    h = x_ref[...] + _dot(mix, wo_ref[...])
    u = (h * lax.rsqrt(jnp.mean(h * h, axis=-1, keepdims=True) + EPS) * gf_ref[...]).astype(BF16)
    acc = h
    for j in range(D_FF // fc):
        cs = slice(j * fc, (j + 1) * fc)
        gt = _dot(u, wg_ref[:, cs])
        up = _dot(u, wu_ref[:, cs])
        act = (gt * jax.nn.sigmoid(gt) * up).astype(BF16)
        acc = acc + _dot(act, wd_ref[cs, :])
    o_ref[...] = acc


def _tail(x, attn, rw, wts, ts):
    n = x.shape[0]
    tok = lambda w: pl.BlockSpec((ts, w), lambda i: (i, 0))
    wspec = lambda w: pl.BlockSpec(w.shape, lambda i: (0,) * w.ndim, pipeline_mode=pl.Buffered(1))
    return pl.pallas_call(
        functools.partial(_tail_kernel, fc=MXU_N), grid=(n // ts,),
        in_specs=[tok(D_MODEL), tok(MLA_DIM), tok(RWKV_DIM)] + [wspec(w) for w in wts],
        out_specs=tok(D_MODEL),
        out_shape=jax.ShapeDtypeStruct((n, D_MODEL), F32),
        compiler_params=_params(("arbitrary",)), name="tail",
    )(x, attn, rw, *wts)


def _block_matrix(size, seg_of, scale_of):
    seg = np.array([seg_of(i) for i in range(size)])
    m = np.zeros((size, size), np.float32)
    for i in range(size):
        if seg[i] >= 0:
            m[i, seg == seg[i]] = scale_of(i)
    return jnp.asarray(m, BF16)


def _rope_tables(pos):
    half = QK_ROPE // 2
    inv = ROPE_BASE ** (-jnp.arange(half, dtype=F32) / half)
    ang = pos.astype(F32)[:, None] * inv[None, :]
    cos, sin = jnp.cos(ang), jnp.sin(ang)
    t = pos.shape[0]
    cc = jnp.concatenate([cos, cos], axis=1)
    ss = jnp.concatenate([-sin, sin], axis=1)
    zq = jnp.zeros((t, LANE - QK_NOPE - QK_ROPE), F32)
    cosq = jnp.concatenate([jnp.ones((t, QK_NOPE), F32), cc, zq], axis=1)
    sinq = jnp.concatenate([jnp.zeros((t, QK_NOPE), F32), ss, zq], axis=1)
    zk = jnp.zeros((t, LANE - QK_ROPE), F32)
    return cosq, sinq, jnp.concatenate([cc, zk], axis=1), jnp.concatenate([ss, zk], axis=1)


def _prep_weights(norm_mix_g, w_in, q_norm_g, w_q_up, kv_norm_g, w_kv_up, qn_nope_g, qn_rope_g,
                  kn_nope_g, kn_rope_g, mu_shift, w0, w2, a0, a2, k_k, k_a):
    half = QK_ROPE // 2
    swap = np.concatenate([np.arange(half, QK_ROPE), np.arange(half)])
    row = lambda v: v.astype(F32).reshape(1, -1)
    pad_to = lambda v, n: jnp.concatenate([v, jnp.zeros(v.shape[:-1] + (n - v.shape[-1],), v.dtype)], axis=-1)

    mla_cols = Q_LORA + KV_LORA + QK_ROPE
    w_kr = w_in[:, Q_LORA + KV_LORA:mla_cols]
    win = jnp.concatenate([pad_to(jnp.concatenate([w_in[:, :mla_cols], w_kr[:, swap]], axis=1), MLA_EXT),
                           w_in[:, mla_cols:]], axis=1).astype(BF16)

    qh = w_q_up.reshape(Q_LORA, MLA_HEADS, QK_NOPE + QK_ROPE)
    zq = jnp.zeros((Q_LORA, MLA_HEADS, QK_NOPE), w_q_up.dtype)
    wqa = pad_to(qh, HEAD_GROUP).reshape(Q_LORA, QK_COLS).astype(BF16)
    wqb = pad_to(jnp.concatenate([zq, qh[:, :, QK_NOPE:][:, :, swap]], axis=2), HEAD_GROUP)
    wqb = wqb.reshape(Q_LORA, QK_COLS).astype(BF16)
    gqa = jnp.tile(pad_to(jnp.concatenate([qn_nope_g, qn_rope_g]), HEAD_GROUP), MLA_HEADS)
    gqb = jnp.tile(pad_to(jnp.concatenate([jnp.zeros_like(qn_nope_g), qn_rope_g[swap]]), HEAD_GROUP), MLA_HEADS)

    def seg_q(i):
        j = i % HEAD_GROUP
        base = (i // HEAD_GROUP) * 2
        return base if j < QK_NOPE else (base + 1 if j < QK_NOPE + QK_ROPE else -1)
    mq = _block_matrix(MXU_N, seg_q, lambda i: 1.0 / (QK_NOPE if i % HEAD_GROUP < QK_NOPE else QK_ROPE))
    mk = _block_matrix(MXU_N, lambda i: i // HEAD_GROUP if i % HEAD_GROUP < QK_NOPE else -1,
                       lambda i: 1.0 / QK_NOPE)
    m64sum = _block_matrix(MXU_N, lambda i: i // RWKV_HEAD, lambda i: 1.0)
    m64mean = _block_matrix(MXU_N, lambda i: i // RWKV_HEAD, lambda i: 1.0 / RWKV_HEAD)

    kvh = w_kv_up.reshape(KV_LORA, MLA_HEADS, QK_NOPE + V_DIM)
    wkk = pad_to(kvh[:, :, :QK_NOPE], HEAD_GROUP).reshape(KV_LORA, QK_COLS).astype(BF16)
    wv = kvh[:, :, QK_NOPE:].reshape(KV_LORA, MLA_DIM).astype(BF16)
    gkn = jnp.tile(pad_to(kn_nope_g, HEAD_GROUP), MLA_HEADS)
    erep_np = np.zeros((LANE, QK_COLS), np.float32)
    for h in range(MLA_HEADS):
        for j in range(QK_ROPE):
            erep_np[j, h * HEAD_GROUP + QK_NOPE + j] = 1.0
    erep = jnp.asarray(erep_np, BF16)
    gkr = pad_to(kn_rope_g, LANE)
    gkrs = pad_to(kn_rope_g[swap], LANE)

    w2e = jnp.concatenate([w2, jnp.zeros((AAA_LORA, RWKV_DIM), w2.dtype)], axis=0).astype(BF16)
    a2e = jnp.concatenate([jnp.zeros((DECAY_LORA, RWKV_DIM), a2.dtype), a2], axis=0).astype(BF16)

    inproj_w = [row(norm_mix_g), win, row(q_norm_g), wqa, wqb, row(gqa), row(gqb), mq,
                row(kv_norm_g), row(gkr), row(gkrs), wkk, row(gkn), mk, erep, wv, wv.T,
                row(mu_shift), row(w0), w2e, row(a0), a2e, row(k_k), row(k_a), m64sum]
    wkc = kvh[:, :, :QK_NOPE].reshape(KV_LORA, MLA_HEADS * QK_NOPE).astype(BF16)
    mct_np = np.zeros((16, MLA_HEADS * QK_NOPE), np.float32)
    for h in range(MLA_HEADS):
        mct_np[h, h * QK_NOPE:(h + 1) * QK_NOPE] = 1.0 / QK_NOPE
    sample_w = [wkc, row(kn_nope_g), jnp.asarray(mct_np, BF16), wv]
    return inproj_w, sample_w, m64mean, m64sum


def kernel(x_prompt, x_sample, cache_kv_latent, cache_k_rope, state_wkv, state_shift, meta_tokens, norm_mix_g, w_in, q_norm_g, w_q_up, kv_norm_g, w_kv_up, qn_nope_g, qn_rope_g, kn_nope_g, kn_rope_g, attn_out_g, mu_shift, w0, w2, a0, a2, g2, k_k, k_a, r_k, lnx_g, lnx_b, w_out, norm_ffn_g, w_gate, w_up, w_down):
    b, s, _ = x_prompt.shape
    db, ds, _ = x_sample.shape
    past = cache_kv_latent.shape[1]
    row = lambda v: v.astype(F32).reshape(1, -1)

    inproj_w, sample_w, m64mean, m64sum = _prep_weights(
        norm_mix_g, w_in, q_norm_g, w_q_up, kv_norm_g, w_kv_up, qn_nope_g, qn_rope_g,
        kn_nope_g, kn_rope_g, mu_shift, w0, w2, a0, a2, k_k, k_a)
    wkv_consts = (g2.astype(BF16), row(r_k), row(lnx_g), row(lnx_b), m64mean, m64sum)
    tail_w = [w_out.astype(BF16), row(norm_ffn_g), w_gate.astype(BF16), w_up.astype(BF16), w_down.astype(BF16)]
    g_attn = row(attn_out_g)

    ts = min(512, s)
    meta = _inproj(meta_tokens.astype(F32)[None], jnp.zeros((1, 1, RWKV_COLS), F32),
                   _rope_tables(jnp.arange(N_META)), inproj_w, N_META, values_transposed=True)
    c_m, kr_m, _, kc_m, vt_m = meta[:5]
    _, s_meta = _wkv(_pad_chunk(meta[5:12]), jnp.zeros((1, RWKV_HEADS, RWKV_HEAD, RWKV_HEAD), F32),
                     wkv_consts, CHUNK)

    pr = _inproj(x_prompt, meta[12], _rope_tables(N_META + jnp.arange(s)), inproj_w, ts, values_transposed=True)
    c_p, kr_p, q_p, kc_p, vt_p = pr[:5]
    rw_p, wkv_p = _wkv(pr[5:12], s_meta, wkv_consts, min(4 * CHUNK, s))
    attn_p = _attn_prompt(q_p, kc_p, vt_p, kc_m[0], vt_m[0], g_attn, min(512, s))
    y_prompt = _tail(x_prompt.reshape(b * s, D_MODEL), attn_p.reshape(b * s, MLA_DIM),
                     rw_p.reshape(b * s, RWKV_DIM), tail_w, ts).reshape(b, s, D_MODEL)
    kv_latent_p = jnp.concatenate([jnp.broadcast_to(c_m, (b, N_META, KV_LORA)), c_p], axis=1)
    k_rope_p = jnp.concatenate([jnp.broadcast_to(kr_m, (b, N_META, QK_ROPE)), kr_p], axis=1)

    sm = _inproj(x_sample, state_shift.astype(F32)[:, None, :], _rope_tables(past + jnp.arange(ds)),
                 inproj_w, ds)
    c_s, kr_s, q_s, kc_s, vc_s = sm[:5]
    rw_s, wkv_s = _wkv(_pad_chunk(sm[5:12]), state_wkv.astype(F32), wkv_consts, CHUNK)
    rw_s = rw_s[:, :ds]
    attn_s = _attn_sample(q_s, c_s, kr_s, cache_kv_latent, cache_k_rope, sample_w, g_attn, min(1024, past))
    y_sample = _tail(x_sample.reshape(db * ds, D_MODEL), attn_s.reshape(db * ds, MLA_DIM),
                     rw_s.reshape(db * ds, RWKV_DIM), tail_w, min(512, db * ds)).reshape(db, ds, D_MODEL)

    return (y_prompt, y_sample, kv_latent_p, k_rope_p, wkv_p, pr[12][:, 0, :],
            c_s, kr_s, wkv_s, sm[12][:, 0, :])
```

```python
import functools

import numpy as np
import jax
import jax.numpy as jnp
from jax import lax
from jax.experimental import pallas as pl
from jax.experimental.pallas import tpu as pltpu

F32 = jnp.float32
BF16 = jnp.bfloat16

D_MODEL = 1024
CHUNK = 64
N_META = 16
EPS = 1e-6
NEG = -1e30
MLA_HEADS = 8
QK_NOPE = 64
QK_ROPE = 32
V_DIM = 64
Q_LORA = 256
KV_LORA = 128
ROPE_BASE = 10000.0
ATTN_SCALE = (QK_NOPE + QK_ROPE) ** -0.5
LOG2E = float(np.log2(np.e))
MLA_DIM = MLA_HEADS * V_DIM
RWKV_HEADS = 8
RWKV_HEAD = 64
RWKV_DIM = RWKV_HEADS * RWKV_HEAD
DECAY_LORA = 64
AAA_LORA = 64
GATE_LORA = 128
LNX_EPS = 64e-5
RWKV_COLS = 3 * RWKV_DIM + DECAY_LORA + AAA_LORA + GATE_LORA
D_FF = -(-8 * D_MODEL // (3 * 256)) * 256

LANE = 128
MXU_N = 256
HEAD_GROUP = LANE
QK_COLS = MLA_HEADS * HEAD_GROUP
MLA_EXT = 512
IN_EXT = MLA_EXT + RWKV_COLS
VMEM_LIMIT = 56 * 1024 * 1024


def _dot(a, b):
    return jnp.dot(a, b, preferred_element_type=F32)


def _dot_nt(a, b):
    return lax.dot_general(a, b, (((1,), (1,)), ((), ())), preferred_element_type=F32)


def _dot_tn(a, b):
    return lax.dot_general(a, b, (((0,), (0,)), ((), ())), preferred_element_type=F32)


def _seg_reduce(x, m, terms=2):
    parts = [x.astype(BF16)]
    if terms == 2:
        parts.append((x - parts[0].astype(F32)).astype(BF16))
    outs = []
    for j in range(x.shape[1] // MXU_N):
        sl = slice(j * MXU_N, (j + 1) * MXU_N)
        outs.append(sum(_dot(part[:, sl], m) for part in parts))
    return outs[0] if len(outs) == 1 else jnp.concatenate(outs, axis=1)


def _full_spec(arr):
    nd = arr.ndim
    return pl.BlockSpec(arr.shape, lambda *_: (0,) * nd)


def _params(sem):
    return pltpu.CompilerParams(dimension_semantics=sem, vmem_limit_bytes=VMEM_LIMIT)


def _inproj_kernel(x_ref, prev0_ref, cosq_ref, sinq_ref, cosk_ref, sink_ref,
                   gmix_ref, win_ref, gql_ref, wqa_ref, wqb_ref, gqa_ref, gqb_ref, mq_ref,
                   gkv_ref, gkr_ref, gkrs_ref, wkk_ref, gkn_ref, mk_ref, erep_ref, wv_ref, wvt_ref,
                   mu_ref, w0_ref, w2_ref, a0_ref, a2_ref, kkg_ref, ka_ref, m64_ref,
                   cm_ref, krm_ref,
                   c_ref, kr_ref, q_ref, kc_ref, vc_ref,
                   r_ref, lw_ref, k_ref, v_ref, kk_ref, b_ref, gl_ref, shift_ref,
                   carry_ref, *, values_transposed, lead):
    i = pl.program_id(1)
    ts = x_ref.shape[1]

    def put_rows(ref, val):
        if lead:
            ref[0, pl.ds(pl.multiple_of(lead + i * ts, 8), ts), :] = val
        else:
            ref[0] = val

    def put_padded(ref, val):
        if ref.shape[1] == ts:
            ref[0] = val
        else:
            ref[0, :ts, :] = val
            ref[0, ts:, :] = jnp.zeros((ref.shape[1] - ts, val.shape[1]), val.dtype)

    if lead:
        @pl.when(i == 0)
        def _():
            c_ref[0, :lead, :] = cm_ref[...]
            kr_ref[0, :lead, :] = krm_ref[...]

    x = x_ref[0]
    xn = x * lax.rsqrt(jnp.mean(x * x, axis=-1, keepdims=True) + EPS) * gmix_ref[...]
    p = _dot(xn.astype(BF16), win_ref[...])

    pq = p[:, :Q_LORA]
    ql = pq * lax.rsqrt(jnp.mean(pq * pq, axis=-1, keepdims=True) + EPS) * gql_ref[...]
    ql = ql.astype(BF16)
    qa = _dot(ql, wqa_ref[...])
    qb = _dot(ql, wqb_ref[...])
    qs = lax.rsqrt(_seg_reduce(qa * qa, mq_ref[...], terms=1) + EPS)
    cosq = jnp.concatenate([cosq_ref[...]] * MLA_HEADS, axis=1)
    sinq = jnp.concatenate([sinq_ref[...]] * MLA_HEADS, axis=1)
    q = qs * (qa * gqa_ref[...] * cosq + qb * gqb_ref[...] * sinq) * (ATTN_SCALE * LOG2E)
    q_ref[0] = q.astype(BF16)

    pc = p[:, Q_LORA:Q_LORA + KV_LORA]
    c = pc * lax.rsqrt(jnp.mean(pc * pc, axis=-1, keepdims=True) + EPS) * gkv_ref[...]
    put_rows(c_ref, c)
    cb = c.astype(BF16)

    pk = p[:, Q_LORA + KV_LORA:MLA_EXT]
    lane = lax.broadcasted_iota(jnp.int32, pk.shape, 1)
    ssk = jnp.sum(jnp.where(lane < QK_ROPE, pk * pk, 0.0), axis=-1, keepdims=True)
    sk = lax.rsqrt(ssk * (1.0 / QK_ROPE) + EPS)
    pk_sw = pltpu.roll(pk, LANE - QK_ROPE, axis=1)
    kr = sk * (pk * gkr_ref[...] * cosk_ref[...] + pk_sw * gkrs_ref[...] * sink_ref[...])
    put_rows(kr_ref, kr[:, :QK_ROPE])

    kraw = _dot(cb, wkk_ref[...])
    kn = kraw * lax.rsqrt(_seg_reduce(kraw * kraw, mk_ref[...], terms=1) + EPS) * gkn_ref[...]
    kc_ref[0] = (kn + _dot(kr.astype(BF16), erep_ref[...])).astype(BF16)
    if values_transposed:
        vc_ref[0] = _dot_nt(wvt_ref[...], cb).astype(BF16)
    else:
        vc_ref[0] = _dot(cb, wv_ref[...]).astype(BF16)

    prw = p[:, MLA_EXT:]

    @pl.when(i == 0)
    def _():
        carry_ref[...] = prev0_ref[0]

    row = lax.broadcasted_iota(jnp.int32, (ts, 1), 0)
    shifted = jnp.where(row == 0, carry_ref[...], pltpu.roll(prw, 1, axis=0))
    last = prw[ts - 1:ts, :]
    carry_ref[...] = last
    shift_ref[0] = last
    xm = prw + (shifted - prw) * mu_ref[...]

    o = RWKV_DIM
    r = xm[:, :o]
    k = xm[:, o:2 * o]
    v = xm[:, 2 * o:3 * o]
    wa = xm[:, 3 * o:3 * o + DECAY_LORA + AAA_LORA]
    put_padded(gl_ref, xm[:, 3 * o + DECAY_LORA + AAA_LORA:])
    dw = _dot(jnp.tanh(wa).astype(BF16), w2_ref[...])
    da = _dot(wa.astype(BF16), a2_ref[...])
    put_padded(lw_ref, -jax.nn.sigmoid(w0_ref[...] + dw) * float(np.exp(-0.5)))
    a = jax.nn.sigmoid(a0_ref[...] + da)
    kx = k * kkg_ref[...]
    kk = kx * lax.rsqrt(jnp.maximum(_seg_reduce(kx * kx, m64_ref[...], terms=1), 1e-24))
    put_padded(r_ref, r)
    put_padded(k_ref, k * (1.0 + (a - 1.0) * ka_ref[...]))
    put_padded(v_ref, v)
    put_padded(kk_ref, kk)
    put_padded(b_ref, kk * a)


def _inproj(x, prev0, tabs, wts, ts, values_transposed=False, lead_rows=None):
    b, t, _ = x.shape
    nb = prev0.shape[0]
    assert t % ts == 0
    grid = (b, t // ts)
    tok = lambda w: pl.BlockSpec((1, ts, w), lambda bi, i: (bi, i, 0))
    lead = 0 if lead_rows is None else lead_rows[0].shape[0]
    if lead_rows is None:
        lead_rows = (jnp.zeros((8, KV_LORA), F32), jnp.zeros((8, QK_ROPE), F32))
    tpad = t if t % CHUNK == 0 else CHUNK
    assert tpad == t or t == ts
    in_specs = [tok(D_MODEL),
                pl.BlockSpec((1, 1, RWKV_COLS), (lambda bi, i: (bi, 0, 0)) if nb > 1 else (lambda bi, i: (0, 0, 0)))]
    in_specs += [pl.BlockSpec((ts, LANE), lambda bi, i: (i, 0)) for _ in tabs]
    in_specs += [_full_spec(w) for w in wts] + [_full_spec(w) for w in lead_rows]
    widths = [(KV_LORA, F32), (QK_ROPE, F32), (QK_COLS, BF16), (QK_COLS, BF16), (MLA_DIM, BF16)]
    widths += [(RWKV_DIM, F32)] * 6 + [(GATE_LORA, F32)]
    out_shape = [jax.ShapeDtypeStruct((b, t, w), dt) for w, dt in widths]
    out_specs = [tok(w) for w, _ in widths]
    if values_transposed:
        out_shape[4] = jax.ShapeDtypeStruct((b, MLA_DIM, t), BF16)
        out_specs[4] = pl.BlockSpec((1, MLA_DIM, ts), lambda bi, i: (bi, 0, i))
    if lead:
        for n in range(2):
            w, dt = widths[n]
            out_shape[n] = jax.ShapeDtypeStruct((b, lead + t, w), dt)
            out_specs[n] = pl.BlockSpec((1, lead + t, w), lambda bi, i: (bi, 0, 0))
    if tpad != t:
        for n in range(5, 12):
            w, dt = widths[n]
            out_shape[n] = jax.ShapeDtypeStruct((b, tpad, w), dt)
            out_specs[n] = pl.BlockSpec((1, tpad, w), lambda bi, i: (bi, 0, 0))
    out_shape.append(jax.ShapeDtypeStruct((b, 1, RWKV_COLS), F32))
    out_specs.append(pl.BlockSpec((1, 1, RWKV_COLS), lambda bi, i: (bi, 0, 0)))
    return pl.pallas_call(
        functools.partial(_inproj_kernel, values_transposed=values_transposed, lead=lead), grid=grid, in_specs=in_specs, out_specs=out_specs, out_shape=out_shape,
        scratch_shapes=[pltpu.VMEM((1, RWKV_COLS), F32)],
        compiler_params=_params(("arbitrary", "arbitrary")), name="inproj",
    )(x, prev0, *tabs, *wts, *lead_rows)


PAIRS = RWKV_HEADS // 2
PAIR_W = 2 * RWKV_HEAD


def _wkv_kernel(r_ref, lw_ref, k_ref, v_ref, kk_ref, b_ref, gl_ref, s0_ref,
                tri_ref, g2_ref, rk_ref, lng_ref, lnb_ref, m64mean_ref, m64sum_ref,
                rw_ref, sout_ref, state_ref):
    i = pl.program_id(1)
    nb, tb = r_ref.shape[:2]
    tt = nb * tb
    c = CHUNK
    nch = tt // c
    cps = tb // c

    @pl.when(i == 0)
    def _():
        state_ref[...] = jnp.broadcast_to(s0_ref[...], state_ref.shape)

    rows_of = lambda ref: ref[0] if nb == 1 else jnp.concatenate([ref[bi] for bi in range(nb)], axis=0)
    r = rows_of(r_ref)
    lw = rows_of(lw_ref)
    k = rows_of(k_ref)
    v = rows_of(v_ref)
    tri = tri_ref[...]
    l1 = lw.astype(BF16)
    e1 = lw - l1.astype(F32)
    l2 = e1.astype(BF16)
    l3 = (e1 - l2.astype(F32)).astype(BF16)
    psum = lambda x: _dot(tri, x[:tb]) if nb == 1 else jnp.concatenate(
        [_dot(tri, x[bi * tb:(bi + 1) * tb]) for bi in range(nb)], axis=0)
    cum = psum(l1) + psum(l2) + psum(l3)
    g = jnp.exp(cum)
    gi = jnp.exp(-cum)
    gp = jnp.exp(cum - lw)
    gc_rows = [g[j * c + c - 1:(j + 1) * c, :] for j in range(nch)]
    gcb = jnp.concatenate([jnp.broadcast_to(x, (c, RWKV_DIM)) for x in gc_rows], axis=0)
    at = -(rows_of(kk_ref) * gp)
    bt = rows_of(b_ref) * gi
    kt = k * gi
    rt = r * g

    even = (lax.broadcasted_iota(jnp.int32, (tt, RWKV_DIM), 1) % PAIR_W) < RWKV_HEAD
    split = lambda x: (jnp.where(even, x, 0.0).astype(BF16), jnp.where(even, 0.0, x).astype(BF16))
    atm_b, rtm_b, vm_b = split(at), split(rt), split(v)
    bt_b, kt_b, v_b = bt.astype(BF16), kt.astype(BF16), v.astype(BF16)
    btc_b, ktc_b = (bt * gcb).astype(BF16), (kt * gcb).astype(BF16)

    lane = lax.broadcasted_iota(jnp.int32, (c, PAIR_W), 1)
    lo = lane < c
    eye_hi = (lane == lax.broadcasted_iota(jnp.int32, (c, PAIR_W), 0) + c).astype(F32)
    rowi = lax.broadcasted_iota(jnp.int32, (2 * c, 2 * c), 0)
    coli = lax.broadcasted_iota(jnp.int32, (2 * c, 2 * c), 1)
    colt = jnp.where(coli >= c, coli - c, coli)
    gmask = colt < jnp.where(rowi < c, rowi, rowi - c + 1)
    blk = (rowi // c) == (coli // c)
    z1 = jnp.zeros((c, PAIR_W), BF16)
    z2 = jnp.zeros((c, 2 * PAIR_W), BF16)

    items = [(j, h) for j in range(nch) for h in range(RWKV_HEADS)]
    rows = lambda j: slice(j * c, (j + 1) * c)
    lanes = lambda h: slice((h // 2) * PAIR_W, (h // 2 + 1) * PAIR_W)
    cut = lambda x, j, h: x[rows(j), lanes(h)]
    vstack = lambda a, b: jnp.concatenate([a, b], axis=0)
    hstack = lambda a, b: jnp.concatenate([a, b], axis=1)

    gb, xs = [], []
    for j, h in items:
        ar = vstack(cut(atm_b[h % 2], j, h), cut(rtm_b[h % 2], j, h))
        bk = vstack(cut(bt_b, j, h), cut(kt_b, j, h))
        gm = jnp.where(gmask, _dot_nt(ar, bk), 0.0)
        gb.append(gm.astype(BF16))
        xs.append(jnp.where(lo, gm[:c], eye_hi))
    for _ in range(int(np.log2(c)) - 1):
        xs = [_dot(x[:, :c].astype(BF16), x.astype(BF16)) + jnp.where(lo, 0.0, x) for x in xs]
    xb = [x.astype(BF16) for x in xs]
    vm = [cut(vm_b[h % 2], j, h) for j, h in items]
    akv = [_dot(gb[n][:c], vstack(z1, vm[n])) for n in range(len(items))]
    zc = [hstack(cut(atm_b[h % 2], j, h), akv[n].astype(BF16)) for n, (j, h) in enumerate(items)]
    w = [_dot(xb[n], vstack(z2, zc[n])) for n in range(len(items))]
    tu = [w[n] + _dot(xb[n], vstack(w[n].astype(BF16), z2)) for n in range(len(items))]
    ry = [_dot(gb[n][c:], vstack(tu[n].astype(BF16), hstack(z1, vm[n]))) for n in range(len(items))]

    reff, yv, pm, qm = {}, {}, {}, {}
    for j in range(nch):
        for p in range(PAIRS):
            n0 = j * RWKV_HEADS + 2 * p
            tu_p = tu[n0] + tu[n0 + 1]
            ry_p = ry[n0] + ry[n0 + 1]
            h = 2 * p
            reff[j, p] = (cut(rt, j, h) + ry_p[:, :PAIR_W]).astype(BF16)
            yv[j, p] = ry_p[:, PAIR_W:]
            pm[j, p] = jnp.where(blk, _dot_tn(tu_p[:, :PAIR_W].astype(BF16), cut(btc_b, j, h)), 0.0).astype(BF16)
            qm[j, p] = jnp.where(blk, _dot_tn(vstack(tu_p[:, PAIR_W:].astype(BF16), cut(v_b, j, h)),
                                              vstack(cut(btc_b, j, h), cut(ktc_b, j, h))), 0.0)

    st = [[state_ref[bi, p] for p in range(PAIRS)] for bi in range(nb)]
    ys = [None] * nch
    for jj in range(cps):
        for bi in range(nb):
            j = bi * cps + jj
            sb = [s.astype(BF16) for s in st[bi]]
            ys[j] = jnp.concatenate([_dot_nt(reff[j, p], sb[p]) + yv[j, p] for p in range(PAIRS)], axis=1)
            st[bi] = [st[bi][p] * gc_rows[j][:, p * PAIR_W:(p + 1) * PAIR_W] + _dot(sb[p], pm[j, p]) + qm[j, p]
                      for p in range(PAIRS)]
    for bi in range(nb):
        for p in range(PAIRS):
            state_ref[bi, p] = st[bi][p]
    sout_ref[...] = state_ref[...]

    y = ys[0] if nch == 1 else jnp.concatenate(ys, axis=0)
    mean = _seg_reduce(y, m64mean_ref[...])
    d = y - mean
    var = _seg_reduce(d * d, m64mean_ref[...], terms=1)
    yn = d * lax.rsqrt(var + LNX_EPS) * lng_ref[...] + lnb_ref[...]
    bonus = _seg_reduce(r * k * rk_ref[...], m64sum_ref[...], terms=1) * v
    gate = _dot(jax.nn.sigmoid(rows_of(gl_ref)).astype(BF16), g2_ref[...])
    rw = ((yn + bonus) * gate).astype(rw_ref.dtype)
    for bi in range(nb):
        rw_ref[bi] = rw[bi * tb:(bi + 1) * tb]


def _state_to_pairs(s):
    z = jnp.zeros_like(s[:, 0::2])
    top = jnp.concatenate([s[:, 0::2], z], axis=-1)
    bot = jnp.concatenate([z, s[:, 1::2]], axis=-1)
    return jnp.concatenate([top, bot], axis=2)


def _pairs_to_state(sp):
    b = sp.shape[0]
    s = jnp.stack([sp[:, :, :RWKV_HEAD, :RWKV_HEAD], sp[:, :, RWKV_HEAD:, RWKV_HEAD:]], axis=2)
    return s.reshape(b, RWKV_HEADS, RWKV_HEAD, RWKV_HEAD)


def _wkv(seqs, s0, consts, tt, nb):
    b, t, _ = seqs[0].shape
    assert b % nb == 0 and t % tt == 0 and tt % CHUNK == 0
    tok = lambda w: pl.BlockSpec((nb, tt, w), lambda bi, i: (bi, i, 0))
    st_shape = (nb, PAIRS, PAIR_W, PAIR_W)
    in_specs = [tok(RWKV_DIM)] * 6 + [tok(GATE_LORA)]
    s0_shape = (1,) + st_shape[1:] if s0.shape[0] == 1 else st_shape
    in_specs.append(pl.BlockSpec(s0_shape, (lambda bi, i: (bi, 0, 0, 0)) if s0.shape[0] > 1 else (lambda bi, i: (0, 0, 0, 0))))
    tri = jnp.kron(jnp.eye(tt // CHUNK, dtype=F32), jnp.tril(jnp.ones((CHUNK, CHUNK), F32))).astype(BF16)
    consts = (tri,) + tuple(consts)
    in_specs += [_full_spec(w) for w in consts]
    rw, sp = pl.pallas_call(
        _wkv_kernel, grid=(b // nb, t // tt),
        in_specs=in_specs,
        out_specs=[tok(RWKV_DIM), pl.BlockSpec(st_shape, lambda bi, i: (bi, 0, 0, 0))],
        out_shape=[jax.ShapeDtypeStruct((b, t, RWKV_DIM), BF16),
                   jax.ShapeDtypeStruct((b,) + st_shape[1:], F32)],
        scratch_shapes=[pltpu.VMEM(st_shape, F32)],
        compiler_params=_params(("arbitrary", "arbitrary")), name="wkv",
    )(*seqs, _state_to_pairs(s0), *consts)
    return rw, _pairs_to_state(sp)


_hq = lambda h: slice(h * HEAD_GROUP, (h + 1) * HEAD_GROUP)


L_ROWS = 16
ACC_ROWS = V_DIM + L_ROWS


def _attn_step_t(qh, kt, vt, mask, first, m_ref, acc_ref, qw=slice(None)):
    n = vt(0).shape[1]
    ones = jnp.ones((L_ROWS, n), BF16)
    s = [_dot_nt(kt(h), qh(h)) for h in range(MLA_HEADS)]
    if mask is not None:
        s = [jnp.where(mask, x, NEG) for x in s]
    ps, alphas = [], []
    for h in range(MLA_HEADS):
        mx = jnp.max(s[h], axis=0, keepdims=True)
        if first:
            m_new = mx
        else:
            m_old = m_ref[h, :, qw]
            m_new = jnp.maximum(m_old, mx)
            alphas.append(jnp.exp2(m_old - m_new))
        m_ref[h, :, qw] = m_new
        ps.append(jnp.exp2(s[h] - m_new).astype(BF16))
    for h in range(MLA_HEADS):
        pv = _dot(jnp.concatenate([vt(h), ones], axis=0), ps[h])
        acc_ref[h, :, qw] = pv if first else alphas[h] * acc_ref[h, :, qw] + pv


def _attn_prompt_kernel(q_ref, k_ref, vt_ref, km_ref, vtm_ref, g_ref, o_ref, m_ref, acc_ref, *, tk):
    i = pl.program_id(1)
    tq = q_ref.shape[1]
    assert tq == tk and tq % (2 * CHUNK) == 0
    half = tq // 2
    st = (m_ref, acc_ref)
    qh = lambda h: q_ref[0, :, _hq(h)]
    hd = lambda h: slice(h * V_DIM, (h + 1) * V_DIM)

    _attn_step_t(qh, lambda h: km_ref[:, _hq(h)], lambda h: vtm_ref[hd(h), :], None, True, *st)

    def body(t, carry):
        ks = pl.ds(pl.multiple_of(t * tk, tk), tk)
        _attn_step_t(qh, lambda h: k_ref[0, ks, _hq(h)], lambda h: vt_ref[0, hd(h), ks], None, False, *st)
        return carry

    lax.fori_loop(0, i, body, 0)
    k0 = pl.ds(pl.multiple_of(i * tk, tk), half)
    k1 = pl.ds(pl.multiple_of(i * tk + half, half), half)
    krow = lax.broadcasted_iota(jnp.int32, (half, tq), 0) // CHUNK
    qcol = lax.broadcasted_iota(jnp.int32, (half, tq), 1) // CHUNK
    _attn_step_t(qh, lambda h: k_ref[0, k0, _hq(h)], lambda h: vt_ref[0, hd(h), k0], krow <= qcol, False, *st)
    _attn_step_t(lambda h: q_ref[0, half:, _hq(h)], lambda h: k_ref[0, k1, _hq(h)],
                 lambda h: vt_ref[0, hd(h), k1], (krow <= qcol)[:, :half], False, *st, qw=slice(half, tq))

    o_t = jnp.concatenate([acc_ref[h, :V_DIM, :] / acc_ref[h, V_DIM:V_DIM + 1, :] for h in range(MLA_HEADS)],
                          axis=0)
    o_t = o_t * lax.rsqrt(jnp.mean(o_t * o_t, axis=0, keepdims=True) + EPS)
    o_ref[0] = (o_t.T * g_ref[...]).astype(o_ref.dtype)


def _attn_prompt(q, kc, vt, kmeta, vtmeta, g, tq):
    b, s, _ = q.shape
    return pl.pallas_call(
        functools.partial(_attn_prompt_kernel, tk=tq), grid=(b, s // tq),
        in_specs=[pl.BlockSpec((1, tq, QK_COLS), lambda bi, i: (bi, i, 0)),
                  pl.BlockSpec((1, s, QK_COLS), lambda bi, i: (bi, 0, 0)),
                  pl.BlockSpec((1, MLA_DIM, s), lambda bi, i: (bi, 0, 0)),
                  _full_spec(kmeta), _full_spec(vtmeta), _full_spec(g)],
        out_specs=pl.BlockSpec((1, tq, MLA_DIM), lambda bi, i: (bi, i, 0)),
        out_shape=jax.ShapeDtypeStruct((b, s, MLA_DIM), BF16),
        scratch_shapes=[pltpu.VMEM((MLA_HEADS, 1, tq), F32), pltpu.VMEM((MLA_HEADS, ACC_ROWS, tq), F32)],
        compiler_params=_params(("arbitrary", "arbitrary")), name="attn_prompt",
    )(q, kc, vt, kmeta, vtmeta, g)


def _attn_sample_kernel(q_ref, qr_ref, cn_ref, krn_ref, cc_ref, ck_ref, wkc_ref, gk_ref, mct_ref, wv_ref,
                        g_ref, o_ref, m_ref, l_ref, acc_ref, *, tk, past):
    tq = q_ref.shape[1]
    rows = MLA_HEADS * tq
    qp = jnp.concatenate(
        [_dot_nt((q_ref[0, :, h * HEAD_GROUP:h * HEAD_GROUP + QK_NOPE].astype(F32) * gk_ref[...]).astype(BF16),
                 wkc_ref[:, h * QK_NOPE:(h + 1) * QK_NOPE]) for h in range(MLA_HEADS)], axis=0).astype(BF16)
    qr = qr_ref[0]
    qchunk = (past + lax.broadcasted_iota(jnp.int32, (rows, 1), 0) % tq) // CHUNK

    def step(cb, krb, kpos, first):
        n = cb.shape[0]
        kraw = _dot(cb, wkc_ref[...])
        ss_t = _dot_nt(mct_ref[...], (kraw * kraw).astype(BF16))
        sc_t = lax.rsqrt(ss_t + EPS)
        scale = jnp.concatenate([jnp.broadcast_to(sc_t[h:h + 1, :], (tq, n)) for h in range(MLA_HEADS)], axis=0)
        s = _dot_nt(qp, cb) * scale + _dot_nt(qr, krb)
        kchunk = (kpos + lax.broadcasted_iota(jnp.int32, (1, n), 1)) // CHUNK
        s = jnp.where(kchunk <= qchunk, s, NEG)
        mx = jnp.max(s, axis=-1, keepdims=True)
        if first:
            m_new = mx
        else:
            m_old = m_ref[...]
            m_new = jnp.maximum(m_old, mx)
            alpha = jnp.exp2(m_old - m_new)
        pr = jnp.exp2(s - m_new)
        sm = jnp.sum(pr, axis=-1, keepdims=True)
        pc = _dot(pr.astype(BF16), cb)
        m_ref[...] = m_new
        l_ref[...] = sm if first else alpha * l_ref[...] + sm
        acc_ref[...] = pc if first else alpha * acc_ref[...] + pc

    step(cn_ref[0].astype(BF16), krn_ref[0].astype(BF16), past, True)

    def body(t, carry):
        ks = pl.ds(pl.multiple_of(t * tk, tk), tk)
        step(cc_ref[0, ks, :].astype(BF16), ck_ref[0, ks, :].astype(BF16), t * tk, False)
        return carry

    lax.fori_loop(0, past // tk, body, 0)
    oc = (acc_ref[...] / l_ref[...]).astype(BF16)
    oall = _dot(oc, wv_ref[...])
    lane_head = lax.broadcasted_iota(jnp.int32, (tq, MLA_DIM), 1) // V_DIM
    o = jnp.zeros((tq, MLA_DIM), F32)
    for h in range(MLA_HEADS):
        o = jnp.where(lane_head == h, oall[h * tq:(h + 1) * tq], o)
    o = o * lax.rsqrt(jnp.mean(o * o, axis=-1, keepdims=True) + EPS) * g_ref[...]
    o_ref[0] = o.astype(o_ref.dtype)


def _attn_sample(q, c_new, kr_new, cache_c, cache_kr, wts, g, tk):
    b, t, _ = q.shape
    past = cache_c.shape[1]
    rows = MLA_HEADS * t
    bspec = lambda shp: pl.BlockSpec((1,) + shp, lambda bi: (bi, 0, 0))
    qr = q.reshape(b, t, MLA_HEADS, HEAD_GROUP)[..., QK_NOPE:QK_NOPE + QK_ROPE]
    qr = jnp.swapaxes(qr, 1, 2).reshape(b, rows, QK_ROPE)
    return pl.pallas_call(
        functools.partial(_attn_sample_kernel, tk=tk, past=past), grid=(b,),
        in_specs=[bspec((t, QK_COLS)), bspec((rows, QK_ROPE)), bspec((t, KV_LORA)), bspec((t, QK_ROPE)),
                  bspec((past, KV_LORA)), bspec((past, QK_ROPE))]
                 + [_full_spec(w) for w in wts] + [_full_spec(g)],
        out_specs=bspec((t, MLA_DIM)),
        out_shape=jax.ShapeDtypeStruct((b, t, MLA_DIM), BF16),
        scratch_shapes=[pltpu.VMEM((rows, 1), F32), pltpu.VMEM((rows, 1), F32), pltpu.VMEM((rows, KV_LORA), F32)],
        compiler_params=_params(("arbitrary",)), name="attn_sample",
    )(q, qr, c_new, kr_new, cache_c, cache_kr, *wts, g)


def _tail_kernel(x_ref, at_ref, rw_ref, wo_ref, gf_ref, wg_ref, wu_ref, wd_ref, o_ref, *, fc):
    mix = jnp.concatenate([at_ref[...], rw_ref[...]], axis=1)
    h = x_ref[...] + _dot(mix, wo_ref[...])
    u = (h * lax.rsqrt(jnp.mean(h * h, axis=-1, keepdims=True) + EPS) * gf_ref[...]).astype(BF16)
    acc = h
    for j in range(D_FF // fc):
        cs = slice(j * fc, (j + 1) * fc)
        gt = _dot(u, wg_ref[:, cs])
        up = _dot(u, wu_ref[:, cs])
        act = (gt * jax.nn.sigmoid(gt) * up).astype(BF16)
        acc = acc + _dot(act, wd_ref[cs, :])
    o_ref[...] = acc


def _tail(x, attn, rw, wts, ts):
    n = x.shape[0]
    tok = lambda w: pl.BlockSpec((ts, w), lambda i: (i, 0))
    wspec = lambda w: pl.BlockSpec(w.shape, lambda i: (0,) * w.ndim, pipeline_mode=pl.Buffered(1))
    return pl.pallas_call(
        functools.partial(_tail_kernel, fc=MXU_N), grid=(n // ts,),
        in_specs=[tok(D_MODEL), tok(MLA_DIM), tok(RWKV_DIM)] + [wspec(w) for w in wts],
        out_specs=tok(D_MODEL),
        out_shape=jax.ShapeDtypeStruct((n, D_MODEL), F32),
        compiler_params=_params(("arbitrary",)), name="tail",
    )(x, attn, rw, *wts)


def _block_matrix(size, seg_of, scale_of):
    seg = np.array([seg_of(i) for i in range(size)])
    m = np.zeros((size, size), np.float32)
    for i in range(size):
        if seg[i] >= 0:
            m[i, seg == seg[i]] = scale_of(i)
    return jnp.asarray(m, BF16)


def _rope_tables(pos):
    half = QK_ROPE // 2
    inv = ROPE_BASE ** (-jnp.arange(half, dtype=F32) / half)
    ang = pos.astype(F32)[:, None] * inv[None, :]
    cos, sin = jnp.cos(ang), jnp.sin(ang)
    t = pos.shape[0]
    cc = jnp.concatenate([cos, cos], axis=1)
    ss = jnp.concatenate([-sin, sin], axis=1)
    zq = jnp.zeros((t, LANE - QK_NOPE - QK_ROPE), F32)
    cosq = jnp.concatenate([jnp.ones((t, QK_NOPE), F32), cc, zq], axis=1)
    sinq = jnp.concatenate([jnp.zeros((t, QK_NOPE), F32), ss, zq], axis=1)
    zk = jnp.zeros((t, LANE - QK_ROPE), F32)
    return cosq, sinq, jnp.concatenate([cc, zk], axis=1), jnp.concatenate([ss, zk], axis=1)


def _prep_weights(norm_mix_g, w_in, q_norm_g, w_q_up, kv_norm_g, w_kv_up, qn_nope_g, qn_rope_g,
                  kn_nope_g, kn_rope_g, mu_shift, w0, w2, a0, a2, k_k, k_a):
    half = QK_ROPE // 2
    swap = np.concatenate([np.arange(half, QK_ROPE), np.arange(half)])
    row = lambda v: v.astype(F32).reshape(1, -1)
    pad_to = lambda v, n: jnp.concatenate([v, jnp.zeros(v.shape[:-1] + (n - v.shape[-1],), v.dtype)], axis=-1)

    mla_cols = Q_LORA + KV_LORA + QK_ROPE
    w_kr = w_in[:, Q_LORA + KV_LORA:mla_cols]
    win = jnp.concatenate([pad_to(jnp.concatenate([w_in[:, :mla_cols], w_kr[:, swap]], axis=1), MLA_EXT),
                           w_in[:, mla_cols:]], axis=1).astype(BF16)

    qh = w_q_up.reshape(Q_LORA, MLA_HEADS, QK_NOPE + QK_ROPE)
    zq = jnp.zeros((Q_LORA, MLA_HEADS, QK_NOPE), w_q_up.dtype)
    wqa = pad_to(qh, HEAD_GROUP).reshape(Q_LORA, QK_COLS).astype(BF16)
    wqb = pad_to(jnp.concatenate([zq, qh[:, :, QK_NOPE:][:, :, swap]], axis=2), HEAD_GROUP)
    wqb = wqb.reshape(Q_LORA, QK_COLS).astype(BF16)
    gqa = jnp.tile(pad_to(jnp.concatenate([qn_nope_g, qn_rope_g]), HEAD_GROUP), MLA_HEADS)
    gqb = jnp.tile(pad_to(jnp.concatenate([jnp.zeros_like(qn_nope_g), qn_rope_g[swap]]), HEAD_GROUP), MLA_HEADS)

    def seg_q(i):
        j = i % HEAD_GROUP
        base = (i // HEAD_GROUP) * 2
        return base if j < QK_NOPE else (base + 1 if j < QK_NOPE + QK_ROPE else -1)
    mq = _block_matrix(MXU_N, seg_q, lambda i: 1.0 / (QK_NOPE if i % HEAD_GROUP < QK_NOPE else QK_ROPE))
    mk = _block_matrix(MXU_N, lambda i: i // HEAD_GROUP if i % HEAD_GROUP < QK_NOPE else -1,
                       lambda i: 1.0 / QK_NOPE)
    m64sum = _block_matrix(MXU_N, lambda i: i // RWKV_HEAD, lambda i: 1.0)
    m64mean = _block_matrix(MXU_N, lambda i: i // RWKV_HEAD, lambda i: 1.0 / RWKV_HEAD)

    kvh = w_kv_up.reshape(KV_LORA, MLA_HEADS, QK_NOPE + V_DIM)
    wkk = pad_to(kvh[:, :, :QK_NOPE], HEAD_GROUP).reshape(KV_LORA, QK_COLS).astype(BF16)
    wv = kvh[:, :, QK_NOPE:].reshape(KV_LORA, MLA_DIM).astype(BF16)
    gkn = jnp.tile(pad_to(kn_nope_g, HEAD_GROUP), MLA_HEADS)
    erep_np = np.zeros((LANE, QK_COLS), np.float32)
    for h in range(MLA_HEADS):
        for j in range(QK_ROPE):
            erep_np[j, h * HEAD_GROUP + QK_NOPE + j] = 1.0
    erep = jnp.asarray(erep_np, BF16)
    gkr = pad_to(kn_rope_g, LANE)
    gkrs = pad_to(kn_rope_g[swap], LANE)

    w2e = jnp.concatenate([w2, jnp.zeros((AAA_LORA, RWKV_DIM), w2.dtype)], axis=0).astype(BF16)
    a2e = jnp.concatenate([jnp.zeros((DECAY_LORA, RWKV_DIM), a2.dtype), a2], axis=0).astype(BF16)

    inproj_w = [row(norm_mix_g), win, row(q_norm_g), wqa, wqb, row(gqa), row(gqb), mq,
                row(kv_norm_g), row(gkr), row(gkrs), wkk, row(gkn), mk, erep, wv, wv.T,
                row(mu_shift), row(w0), w2e, row(a0), a2e, row(k_k), row(k_a), m64sum]
    wkc = kvh[:, :, :QK_NOPE].reshape(KV_LORA, MLA_HEADS * QK_NOPE).astype(BF16)
    mct_np = np.zeros((16, MLA_HEADS * QK_NOPE), np.float32)
    for h in range(MLA_HEADS):
        mct_np[h, h * QK_NOPE:(h + 1) * QK_NOPE] = 1.0 / QK_NOPE
    sample_w = [wkc, row(kn_nope_g), jnp.asarray(mct_np, BF16), wv]
    return inproj_w, sample_w, m64mean, m64sum


def kernel(x_prompt, x_sample, cache_kv_latent, cache_k_rope, state_wkv, state_shift, meta_tokens, norm_mix_g, w_in, q_norm_g, w_q_up, kv_norm_g, w_kv_up, qn_nope_g, qn_rope_g, kn_nope_g, kn_rope_g, attn_out_g, mu_shift, w0, w2, a0, a2, g2, k_k, k_a, r_k, lnx_g, lnx_b, w_out, norm_ffn_g, w_gate, w_up, w_down):
    b, s, _ = x_prompt.shape
    db, ds, _ = x_sample.shape
    past = cache_kv_latent.shape[1]
    row = lambda v: v.astype(F32).reshape(1, -1)

    inproj_w, sample_w, m64mean, m64sum = _prep_weights(
        norm_mix_g, w_in, q_norm_g, w_q_up, kv_norm_g, w_kv_up, qn_nope_g, qn_rope_g,
        kn_nope_g, kn_rope_g, mu_shift, w0, w2, a0, a2, k_k, k_a)
    wkv_consts = (g2.astype(BF16), row(r_k), row(lnx_g), row(lnx_b), m64mean, m64sum)
    tail_w = [w_out.astype(BF16), row(norm_ffn_g), w_gate.astype(BF16), w_up.astype(BF16), w_down.astype(BF16)]
    g_attn = row(attn_out_g)

    ts = min(512, s)
    meta = _inproj(meta_tokens.astype(F32)[None], jnp.zeros((1, 1, RWKV_COLS), F32),
                   _rope_tables(jnp.arange(N_META)), inproj_w, N_META, values_transposed=True)
    c_m, kr_m, _, kc_m, vt_m = meta[:5]
    _, s_meta = _wkv(meta[5:12], jnp.zeros((1, RWKV_HEADS, RWKV_HEAD, RWKV_HEAD), F32),
                     wkv_consts, CHUNK, 1)

    pr = _inproj(x_prompt, meta[12], _rope_tables(N_META + jnp.arange(s)), inproj_w, ts, values_transposed=True,
                 lead_rows=(c_m[0], kr_m[0]))
    kv_latent_p, k_rope_p, q_p, kc_p, vt_p = pr[:5]
    rw_p, wkv_p = _wkv(pr[5:12], s_meta, wkv_consts, min(4 * CHUNK, s), 2 if b % 2 == 0 else 1)
    attn_p = _attn_prompt(q_p, kc_p, vt_p, kc_m[0], vt_m[0], g_attn, min(512, s))
    y_prompt = _tail(x_prompt.reshape(b * s, D_MODEL), attn_p.reshape(b * s, MLA_DIM),
                     rw_p.reshape(b * s, RWKV_DIM), tail_w, ts).reshape(b, s, D_MODEL)

    sm = _inproj(x_sample, state_shift.astype(F32)[:, None, :], _rope_tables(past + jnp.arange(ds)),
                 inproj_w, ds)
    c_s, kr_s, q_s, kc_s, vc_s = sm[:5]
    rw_s, wkv_s = _wkv(sm[5:12], state_wkv.astype(F32), wkv_consts, CHUNK, 4 if db % 4 == 0 else 1)
    rw_s = rw_s[:, :ds]
    attn_s = _attn_sample(q_s, c_s, kr_s, cache_kv_latent, cache_k_rope, sample_w, g_attn, min(1024, past))
    y_sample = _tail(x_sample.reshape(db * ds, D_MODEL), attn_s.reshape(db * ds, MLA_DIM),
                     rw_s.reshape(db * ds, RWKV_DIM), tail_w, min(512, db * ds)).reshape(db, ds, D_MODEL)

    return (y_prompt, y_sample, kv_latent_p, k_rope_p, wkv_p, pr[12][:, 0, :],
            c_s, kr_s, wkv_s, sm[12][:, 0, :])
```

```python
import functools

import numpy as np
import jax
import jax.numpy as jnp
from jax import lax
from jax.experimental import pallas as pl
from jax.experimental.pallas import tpu as pltpu

F32 = jnp.float32
BF16 = jnp.bfloat16

D_MODEL = 1024
CHUNK = 64
N_META = 16
EPS = 1e-6
NEG = -1e30
MLA_HEADS = 8
QK_NOPE = 64
QK_ROPE = 32
V_DIM = 64
Q_LORA = 256
KV_LORA = 128
ROPE_BASE = 10000.0
ATTN_SCALE = (QK_NOPE + QK_ROPE) ** -0.5
LOG2E = float(np.log2(np.e))
MLA_DIM = MLA_HEADS * V_DIM
RWKV_HEADS = 8
RWKV_HEAD = 64
RWKV_DIM = RWKV_HEADS * RWKV_HEAD
DECAY_LORA = 64
AAA_LORA = 64
GATE_LORA = 128
LNX_EPS = 64e-5
RWKV_COLS = 3 * RWKV_DIM + DECAY_LORA + AAA_LORA + GATE_LORA
D_FF = -(-8 * D_MODEL // (3 * 256)) * 256

LANE = 128
MXU_N = 256
HEAD_GROUP = LANE
QK_COLS = MLA_HEADS * HEAD_GROUP
MLA_EXT = 512
IN_EXT = MLA_EXT + RWKV_COLS
VMEM_LIMIT = 56 * 1024 * 1024

TOKEN_TILE = 512
ATTN_TILE = 512
SAMPLE_KEY_TILE = 1024
WKV_TILE = 4 * CHUNK
WKV_STREAMS = 2
SHORT_STREAMS = 4


def _streams(batch, want):
    return want if batch % want == 0 else 1


def _dot(a, b):
    return jnp.dot(a, b, preferred_element_type=F32)


def _dot_nt(a, b):
    return lax.dot_general(a, b, (((1,), (1,)), ((), ())), preferred_element_type=F32)


def _dot_tn(a, b):
    return lax.dot_general(a, b, (((0,), (0,)), ((), ())), preferred_element_type=F32)


def _seg_reduce(x, m, terms=2):
    parts = [x.astype(BF16)]
    if terms == 2:
        parts.append((x - parts[0].astype(F32)).astype(BF16))
    outs = []
    for j in range(x.shape[1] // MXU_N):
        sl = slice(j * MXU_N, (j + 1) * MXU_N)
        outs.append(sum(_dot(part[:, sl], m) for part in parts))
    return outs[0] if len(outs) == 1 else jnp.concatenate(outs, axis=1)


def _full_spec(arr):
    nd = arr.ndim
    return pl.BlockSpec(arr.shape, lambda *_: (0,) * nd)


def _params(sem):
    return pltpu.CompilerParams(dimension_semantics=sem, vmem_limit_bytes=VMEM_LIMIT)


def _inproj_kernel(x_ref, prev0_ref, cosq_ref, sinq_ref, cosk_ref, sink_ref,
                   gmix_ref, win_ref, gql_ref, wqa_ref, wqb_ref, gqa_ref, gqb_ref, mq_ref,
                   gkv_ref, gkr_ref, gkrs_ref, wkk_ref, gkn_ref, mk_ref, erep_ref, wv_ref, wvt_ref,
                   mu_ref, w0_ref, w2_ref, a0_ref, a2_ref, kkg_ref, ka_ref, m64_ref,
                   cm_ref, krm_ref,
                   c_ref, kr_ref, q_ref, kc_ref, vc_ref,
                   r_ref, lw_ref, k_ref, v_ref, kk_ref, b_ref, gl_ref, shift_ref,
                   carry_ref, *, values_transposed, lead):
    i = pl.program_id(1)
    nb, ts = x_ref.shape[:2]
    part = lambda val, bi: val[bi * ts:(bi + 1) * ts]

    def put(ref, val):
        for bi in range(nb):
            ref[bi] = part(val, bi)

    def put_rows(ref, val):
        if lead:
            ref[0, pl.ds(pl.multiple_of(lead + i * ts, 8), ts), :] = val
        else:
            put(ref, val)

    def put_padded(ref, val):
        if ref.shape[1] == ts:
            put(ref, val)
        else:
            for bi in range(nb):
                ref[bi, :ts, :] = part(val, bi)
                ref[bi, ts:, :] = jnp.zeros((ref.shape[1] - ts, val.shape[1]), val.dtype)

    if lead:
        @pl.when(i == 0)
        def _():
            c_ref[0, :lead, :] = cm_ref[...]
            kr_ref[0, :lead, :] = krm_ref[...]

    x = x_ref[0] if nb == 1 else jnp.concatenate([x_ref[bi] for bi in range(nb)], axis=0)
    tab = lambda ref: ref[...] if nb == 1 else jnp.concatenate([ref[...]] * nb, axis=0)
    xn = x * lax.rsqrt(jnp.mean(x * x, axis=-1, keepdims=True) + EPS) * gmix_ref[...]
    p = _dot(xn.astype(BF16), win_ref[...])

    pq = p[:, :Q_LORA]
    ql = pq * lax.rsqrt(jnp.mean(pq * pq, axis=-1, keepdims=True) + EPS) * gql_ref[...]
    ql = ql.astype(BF16)
    qa = _dot(ql, wqa_ref[...])
    qb = _dot(ql, wqb_ref[...])
    qs = lax.rsqrt(_seg_reduce(qa * qa, mq_ref[...], terms=1) + EPS)
    cosq = jnp.concatenate([tab(cosq_ref)] * MLA_HEADS, axis=1)
    sinq = jnp.concatenate([tab(sinq_ref)] * MLA_HEADS, axis=1)
    q = qs * (qa * gqa_ref[...] * cosq + qb * gqb_ref[...] * sinq) * (ATTN_SCALE * LOG2E)
    put(q_ref, q.astype(BF16))

    pc = p[:, Q_LORA:Q_LORA + KV_LORA]
    c = pc * lax.rsqrt(jnp.mean(pc * pc, axis=-1, keepdims=True) + EPS) * gkv_ref[...]
    put_rows(c_ref, c)
    cb = c.astype(BF16)

    pk = p[:, Q_LORA + KV_LORA:MLA_EXT]
    lane = lax.broadcasted_iota(jnp.int32, pk.shape, 1)
    ssk = jnp.sum(jnp.where(lane < QK_ROPE, pk * pk, 0.0), axis=-1, keepdims=True)
    sk = lax.rsqrt(ssk * (1.0 / QK_ROPE) + EPS)
    pk_sw = pltpu.roll(pk, LANE - QK_ROPE, axis=1)
    kr = sk * (pk * gkr_ref[...] * tab(cosk_ref) + pk_sw * gkrs_ref[...] * tab(sink_ref))
    put_rows(kr_ref, kr[:, :QK_ROPE])

    kraw = _dot(cb, wkk_ref[...])
    kn = kraw * lax.rsqrt(_seg_reduce(kraw * kraw, mk_ref[...], terms=1) + EPS) * gkn_ref[...]
    put(kc_ref, (kn + _dot(kr.astype(BF16), erep_ref[...])).astype(BF16))
    if values_transposed:
        vc_ref[0] = _dot_nt(wvt_ref[...], cb).astype(BF16)
    else:
        put(vc_ref, _dot(cb, wv_ref[...]).astype(BF16))

    prw = p[:, MLA_EXT:]

    @pl.when(i == 0)
    def _():
        carry_ref[...] = jnp.broadcast_to(prev0_ref[...], carry_ref.shape)

    row = lax.broadcasted_iota(jnp.int32, (nb * ts, 1), 0)
    shifted = pltpu.roll(prw, 1, axis=0)
    for bi in range(nb):
        shifted = jnp.where(row == bi * ts, carry_ref[bi], shifted)
        last = prw[(bi + 1) * ts - 1:(bi + 1) * ts, :]
        carry_ref[bi] = last
        shift_ref[bi] = last
    xm = prw + (shifted - prw) * mu_ref[...]

    o = RWKV_DIM
    r = xm[:, :o]
    k = xm[:, o:2 * o]
    v = xm[:, 2 * o:3 * o]
    wa = xm[:, 3 * o:3 * o + DECAY_LORA + AAA_LORA]
    put_padded(gl_ref, xm[:, 3 * o + DECAY_LORA + AAA_LORA:])
    dw = _dot(jnp.tanh(wa).astype(BF16), w2_ref[...])
    da = _dot(wa.astype(BF16), a2_ref[...])
    put_padded(lw_ref, -jax.nn.sigmoid(w0_ref[...] + dw) * float(np.exp(-0.5)))
    a = jax.nn.sigmoid(a0_ref[...] + da)
    kx = k * kkg_ref[...]
    kk = kx * lax.rsqrt(jnp.maximum(_seg_reduce(kx * kx, m64_ref[...], terms=1), 1e-24))
    put_padded(r_ref, r)
    put_padded(k_ref, k * (1.0 + (a - 1.0) * ka_ref[...]))
    put_padded(v_ref, v)
    put_padded(kk_ref, kk)
    put_padded(b_ref, kk * a)


def _inproj(x, prev0, tabs, wts, ts, values_transposed=False, lead_rows=None, nb=1):
    b, t, _ = x.shape
    assert t % ts == 0 and b % nb == 0
    assert nb == 1 or not (values_transposed or lead_rows is not None)
    grid = (b // nb, t // ts)
    tok = lambda w: pl.BlockSpec((nb, ts, w), lambda bi, i: (bi, i, 0))
    lead = 0 if lead_rows is None else lead_rows[0].shape[0]
    if lead_rows is None:
        lead_rows = (jnp.zeros((8, KV_LORA), F32), jnp.zeros((8, QK_ROPE), F32))
    tpad = t if t % CHUNK == 0 else CHUNK
    assert tpad == t or t == ts
    in_specs = [tok(D_MODEL),
                pl.BlockSpec((nb if prev0.shape[0] > 1 else 1, 1, RWKV_COLS),
                             (lambda bi, i: (bi, 0, 0)) if prev0.shape[0] > 1 else (lambda bi, i: (0, 0, 0)))]
    in_specs += [pl.BlockSpec((ts, LANE), lambda bi, i: (i, 0)) for _ in tabs]
    in_specs += [_full_spec(w) for w in wts] + [_full_spec(w) for w in lead_rows]
    widths = [(KV_LORA, F32), (QK_ROPE, F32), (QK_COLS, BF16), (QK_COLS, BF16), (MLA_DIM, BF16)]
    widths += [(RWKV_DIM, F32)] * 6 + [(GATE_LORA, F32)]
    out_shape = [jax.ShapeDtypeStruct((b, t, w), dt) for w, dt in widths]
    out_specs = [tok(w) for w, _ in widths]
    if values_transposed:
        out_shape[4] = jax.ShapeDtypeStruct((b, MLA_DIM, t), BF16)
        out_specs[4] = pl.BlockSpec((1, MLA_DIM, ts), lambda bi, i: (bi, 0, i))
    if lead:
        for n in range(2):
            w, dt = widths[n]
            out_shape[n] = jax.ShapeDtypeStruct((b, lead + t, w), dt)
            out_specs[n] = pl.BlockSpec((1, lead + t, w), lambda bi, i: (bi, 0, 0))
    if tpad != t:
        for n in range(5, 12):
            w, dt = widths[n]
            out_shape[n] = jax.ShapeDtypeStruct((b, tpad, w), dt)
            out_specs[n] = pl.BlockSpec((nb, tpad, w), lambda bi, i: (bi, 0, 0))
    out_shape.append(jax.ShapeDtypeStruct((b, 1, RWKV_COLS), F32))
    out_specs.append(pl.BlockSpec((nb, 1, RWKV_COLS), lambda bi, i: (bi, 0, 0)))
    return pl.pallas_call(
        functools.partial(_inproj_kernel, values_transposed=values_transposed, lead=lead), grid=grid, in_specs=in_specs, out_specs=out_specs, out_shape=out_shape,
        scratch_shapes=[pltpu.VMEM((nb, 1, RWKV_COLS), F32)],
        compiler_params=_params(("arbitrary", "arbitrary")), name="inproj",
    )(x, prev0, *tabs, *wts, *lead_rows)


PAIRS = RWKV_HEADS // 2
PAIR_W = 2 * RWKV_HEAD


def _wkv_kernel(r_ref, lw_ref, k_ref, v_ref, kk_ref, b_ref, gl_ref, s0_ref,
                tri_ref, g2_ref, rk_ref, lng_ref, lnb_ref, m64mean_ref, m64sum_ref,
                rw_ref, sout_ref, state_ref):
    i = pl.program_id(1)
    nb, tb = r_ref.shape[:2]
    tt = nb * tb
    c = CHUNK
    nch = tt // c
    cps = tb // c

    @pl.when(i == 0)
    def _():
        state_ref[...] = jnp.broadcast_to(s0_ref[...], state_ref.shape)

    rows_of = lambda ref: ref[0] if nb == 1 else jnp.concatenate([ref[bi] for bi in range(nb)], axis=0)
    r = rows_of(r_ref)
    lw = rows_of(lw_ref)
    k = rows_of(k_ref)
    v = rows_of(v_ref)
    tri = tri_ref[...]
    l1 = lw.astype(BF16)
    e1 = lw - l1.astype(F32)
    l2 = e1.astype(BF16)
    l3 = (e1 - l2.astype(F32)).astype(BF16)
    psum = lambda x: _dot(tri, x[:tb]) if nb == 1 else jnp.concatenate(
        [_dot(tri, x[bi * tb:(bi + 1) * tb]) for bi in range(nb)], axis=0)
    cum = psum(l1) + psum(l2) + psum(l3)
    g = jnp.exp(cum)
    gi = jnp.exp(-cum)
    gp = jnp.exp(cum - lw)
    gc_rows = [g[j * c + c - 1:(j + 1) * c, :] for j in range(nch)]
    gcb = jnp.concatenate([jnp.broadcast_to(x, (c, RWKV_DIM)) for x in gc_rows], axis=0)
    at = -(rows_of(kk_ref) * gp)
    bt = rows_of(b_ref) * gi
    kt = k * gi
    rt = r * g

    even = (lax.broadcasted_iota(jnp.int32, (tt, RWKV_DIM), 1) % PAIR_W) < RWKV_HEAD
    split = lambda x: (jnp.where(even, x, 0.0).astype(BF16), jnp.where(even, 0.0, x).astype(BF16))
    atm_b, rtm_b, vm_b = split(at), split(rt), split(v)
    bt_b, kt_b, v_b = bt.astype(BF16), kt.astype(BF16), v.astype(BF16)
    btc_b, ktc_b = (bt * gcb).astype(BF16), (kt * gcb).astype(BF16)

    lane = lax.broadcasted_iota(jnp.int32, (c, PAIR_W), 1)
    lo = lane < c
    eye_hi = (lane == lax.broadcasted_iota(jnp.int32, (c, PAIR_W), 0) + c).astype(F32)
    rowi = lax.broadcasted_iota(jnp.int32, (2 * c, 2 * c), 0)
    coli = lax.broadcasted_iota(jnp.int32, (2 * c, 2 * c), 1)
    colt = jnp.where(coli >= c, coli - c, coli)
    gmask = colt < jnp.where(rowi < c, rowi, rowi - c + 1)
    blk = (rowi // c) == (coli // c)
    z1 = jnp.zeros((c, PAIR_W), BF16)
    z2 = jnp.zeros((c, 2 * PAIR_W), BF16)

    items = [(j, h) for j in range(nch) for h in range(RWKV_HEADS)]
    rows = lambda j: slice(j * c, (j + 1) * c)
    lanes = lambda h: slice((h // 2) * PAIR_W, (h // 2 + 1) * PAIR_W)
    cut = lambda x, j, h: x[rows(j), lanes(h)]
    vstack = lambda a, b: jnp.concatenate([a, b], axis=0)
    hstack = lambda a, b: jnp.concatenate([a, b], axis=1)

    gb, xs = [], []
    for j, h in items:
        ar = vstack(cut(atm_b[h % 2], j, h), cut(rtm_b[h % 2], j, h))
        bk = vstack(cut(bt_b, j, h), cut(kt_b, j, h))
        gm = jnp.where(gmask, _dot_nt(ar, bk), 0.0)
        gb.append(gm.astype(BF16))
        xs.append(jnp.where(lo, gm[:c], eye_hi))
    for _ in range(int(np.log2(c)) - 1):
        xs = [_dot(x[:, :c].astype(BF16), x.astype(BF16)) + jnp.where(lo, 0.0, x) for x in xs]
    xb = [x.astype(BF16) for x in xs]
    vm = [cut(vm_b[h % 2], j, h) for j, h in items]
    akv = [_dot(gb[n][:c], vstack(z1, vm[n])) for n in range(len(items))]
    zc = [hstack(cut(atm_b[h % 2], j, h), akv[n].astype(BF16)) for n, (j, h) in enumerate(items)]
    w = [_dot(xb[n], vstack(z2, zc[n])) for n in range(len(items))]
    tu = [w[n] + _dot(xb[n], vstack(w[n].astype(BF16), z2)) for n in range(len(items))]
    ry = [_dot(gb[n][c:], vstack(tu[n].astype(BF16), hstack(z1, vm[n]))) for n in range(len(items))]

    reff, yv, pm, qm = {}, {}, {}, {}
    for j in range(nch):
        for p in range(PAIRS):
            n0 = j * RWKV_HEADS + 2 * p
            tu_p = tu[n0] + tu[n0 + 1]
            ry_p = ry[n0] + ry[n0 + 1]
            h = 2 * p
            reff[j, p] = (cut(rt, j, h) + ry_p[:, :PAIR_W]).astype(BF16)
            yv[j, p] = ry_p[:, PAIR_W:]
            pm[j, p] = jnp.where(blk, _dot_tn(tu_p[:, :PAIR_W].astype(BF16), cut(btc_b, j, h)), 0.0).astype(BF16)
            qm[j, p] = jnp.where(blk, _dot_tn(vstack(tu_p[:, PAIR_W:].astype(BF16), cut(v_b, j, h)),
                                              vstack(cut(btc_b, j, h), cut(ktc_b, j, h))), 0.0)

    st = [[state_ref[bi, p] for p in range(PAIRS)] for bi in range(nb)]
    ys = [None] * nch
    for jj in range(cps):
        for bi in range(nb):
            j = bi * cps + jj
            sb = [s.astype(BF16) for s in st[bi]]
            ys[j] = jnp.concatenate([_dot_nt(reff[j, p], sb[p]) + yv[j, p] for p in range(PAIRS)], axis=1)
            st[bi] = [st[bi][p] * gc_rows[j][:, p * PAIR_W:(p + 1) * PAIR_W] + _dot(sb[p], pm[j, p]) + qm[j, p]
                      for p in range(PAIRS)]
    for bi in range(nb):
        for p in range(PAIRS):
            state_ref[bi, p] = st[bi][p]
    sout_ref[...] = state_ref[...]

    y = ys[0] if nch == 1 else jnp.concatenate(ys, axis=0)
    mean = _seg_reduce(y, m64mean_ref[...])
    d = y - mean
    var = _seg_reduce(d * d, m64mean_ref[...], terms=1)
    yn = d * lax.rsqrt(var + LNX_EPS) * lng_ref[...] + lnb_ref[...]
    bonus = _seg_reduce(r * k * rk_ref[...], m64sum_ref[...], terms=1) * v
    gate = _dot(jax.nn.sigmoid(rows_of(gl_ref)).astype(BF16), g2_ref[...])
    rw = ((yn + bonus) * gate).astype(rw_ref.dtype)
    for bi in range(nb):
        rw_ref[bi] = rw[bi * tb:(bi + 1) * tb]


def _state_to_pairs(s):
    z = jnp.zeros_like(s[:, 0::2])
    top = jnp.concatenate([s[:, 0::2], z], axis=-1)
    bot = jnp.concatenate([z, s[:, 1::2]], axis=-1)
    return jnp.concatenate([top, bot], axis=2)


def _pairs_to_state(sp):
    b = sp.shape[0]
    s = jnp.stack([sp[:, :, :RWKV_HEAD, :RWKV_HEAD], sp[:, :, RWKV_HEAD:, RWKV_HEAD:]], axis=2)
    return s.reshape(b, RWKV_HEADS, RWKV_HEAD, RWKV_HEAD)


def _wkv(seqs, s0, consts, tt, nb):
    b, t, _ = seqs[0].shape
    assert b % nb == 0 and t % tt == 0 and tt % CHUNK == 0
    tok = lambda w: pl.BlockSpec((nb, tt, w), lambda bi, i: (bi, i, 0))
    st_shape = (nb, PAIRS, PAIR_W, PAIR_W)
    in_specs = [tok(RWKV_DIM)] * 6 + [tok(GATE_LORA)]
    s0_shape = (1,) + st_shape[1:] if s0.shape[0] == 1 else st_shape
    in_specs.append(pl.BlockSpec(s0_shape, (lambda bi, i: (bi, 0, 0, 0)) if s0.shape[0] > 1 else (lambda bi, i: (0, 0, 0, 0))))
    tri = jnp.kron(jnp.eye(tt // CHUNK, dtype=F32), jnp.tril(jnp.ones((CHUNK, CHUNK), F32))).astype(BF16)
    consts = (tri,) + tuple(consts)
    in_specs += [_full_spec(w) for w in consts]
    rw, sp = pl.pallas_call(
        _wkv_kernel, grid=(b // nb, t // tt),
        in_specs=in_specs,
        out_specs=[tok(RWKV_DIM), pl.BlockSpec(st_shape, lambda bi, i: (bi, 0, 0, 0))],
        out_shape=[jax.ShapeDtypeStruct((b, t, RWKV_DIM), BF16),
                   jax.ShapeDtypeStruct((b,) + st_shape[1:], F32)],
        scratch_shapes=[pltpu.VMEM(st_shape, F32)],
        compiler_params=_params(("arbitrary", "arbitrary")), name="wkv",
    )(*seqs, _state_to_pairs(s0), *consts)
    return rw, _pairs_to_state(sp)


_hq = lambda h: slice(h * HEAD_GROUP, (h + 1) * HEAD_GROUP)


L_ROWS = 16
ACC_ROWS = V_DIM + L_ROWS


def _attn_step_t(qh, kt, vt, mask, first, m_ref, acc_ref, qw=slice(None)):
    n = vt(0).shape[1]
    ones = jnp.ones((L_ROWS, n), BF16)
    s = [_dot_nt(kt(h), qh(h)) for h in range(MLA_HEADS)]
    if mask is not None:
        s = [jnp.where(mask, x, NEG) for x in s]
    ps, alphas = [], []
    for h in range(MLA_HEADS):
        mx = jnp.max(s[h], axis=0, keepdims=True)
        if first:
            m_new = mx
        else:
            m_old = m_ref[h, :, qw]
            m_new = jnp.maximum(m_old, mx)
            alphas.append(jnp.exp2(m_old - m_new))
        m_ref[h, :, qw] = m_new
        ps.append(jnp.exp2(s[h] - m_new).astype(BF16))
    for h in range(MLA_HEADS):
        pv = _dot(jnp.concatenate([vt(h), ones], axis=0), ps[h])
        acc_ref[h, :, qw] = pv if first else alphas[h] * acc_ref[h, :, qw] + pv


def _attn_prompt_kernel(q_ref, k_ref, vt_ref, km_ref, vtm_ref, g_ref, o_ref, m_ref, acc_ref, *, tk):
    i = pl.program_id(1)
    tq = q_ref.shape[1]
    assert tq == tk and tq % (2 * CHUNK) == 0
    half = tq // 2
    st = (m_ref, acc_ref)
    qh = lambda h: q_ref[0, :, _hq(h)]
    hd = lambda h: slice(h * V_DIM, (h + 1) * V_DIM)

    _attn_step_t(qh, lambda h: km_ref[:, _hq(h)], lambda h: vtm_ref[hd(h), :], None, True, *st)

    def body(t, carry):
        ks = pl.ds(pl.multiple_of(t * tk, tk), tk)
        _attn_step_t(qh, lambda h: k_ref[0, ks, _hq(h)], lambda h: vt_ref[0, hd(h), ks], None, False, *st)
        return carry

    lax.fori_loop(0, i, body, 0)
    k0 = pl.ds(pl.multiple_of(i * tk, tk), half)
    k1 = pl.ds(pl.multiple_of(i * tk + half, half), half)
    krow = lax.broadcasted_iota(jnp.int32, (half, tq), 0) // CHUNK
    qcol = lax.broadcasted_iota(jnp.int32, (half, tq), 1) // CHUNK
    _attn_step_t(qh, lambda h: k_ref[0, k0, _hq(h)], lambda h: vt_ref[0, hd(h), k0], krow <= qcol, False, *st)
    _attn_step_t(lambda h: q_ref[0, half:, _hq(h)], lambda h: k_ref[0, k1, _hq(h)],
                 lambda h: vt_ref[0, hd(h), k1], (krow <= qcol)[:, :half], False, *st, qw=slice(half, tq))

    o_t = jnp.concatenate([acc_ref[h, :V_DIM, :] / acc_ref[h, V_DIM:V_DIM + 1, :] for h in range(MLA_HEADS)],
                          axis=0)
    o_t = o_t * lax.rsqrt(jnp.mean(o_t * o_t, axis=0, keepdims=True) + EPS)
    o_ref[0] = (o_t.T * g_ref[...]).astype(o_ref.dtype)


def _attn_prompt(q, kc, vt, kmeta, vtmeta, g, tq):
    b, s, _ = q.shape
    assert s % tq == 0
    return pl.pallas_call(
        functools.partial(_attn_prompt_kernel, tk=tq), grid=(b, s // tq),
        in_specs=[pl.BlockSpec((1, tq, QK_COLS), lambda bi, i: (bi, i, 0)),
                  pl.BlockSpec((1, s, QK_COLS), lambda bi, i: (bi, 0, 0)),
                  pl.BlockSpec((1, MLA_DIM, s), lambda bi, i: (bi, 0, 0)),
                  _full_spec(kmeta), _full_spec(vtmeta), _full_spec(g)],
        out_specs=pl.BlockSpec((1, tq, MLA_DIM), lambda bi, i: (bi, i, 0)),
        out_shape=jax.ShapeDtypeStruct((b, s, MLA_DIM), BF16),
        scratch_shapes=[pltpu.VMEM((MLA_HEADS, 1, tq), F32), pltpu.VMEM((MLA_HEADS, ACC_ROWS, tq), F32)],
        compiler_params=_params(("arbitrary", "arbitrary")), name="attn_prompt",
    )(q, kc, vt, kmeta, vtmeta, g)


def _attn_sample_kernel(q_ref, qr_ref, cn_ref, krn_ref, cc_ref, ck_ref, wkc_ref, gk_ref, mct_ref, wv_ref,
                        g_ref, o_ref, m_ref, l_ref, acc_ref, *, tk, past):
    tq = q_ref.shape[1]
    rows = MLA_HEADS * tq
    qp = jnp.concatenate(
        [_dot_nt((q_ref[0, :, h * HEAD_GROUP:h * HEAD_GROUP + QK_NOPE].astype(F32) * gk_ref[...]).astype(BF16),
                 wkc_ref[:, h * QK_NOPE:(h + 1) * QK_NOPE]) for h in range(MLA_HEADS)], axis=0).astype(BF16)
    qr = qr_ref[0]
    qchunk = (past + lax.broadcasted_iota(jnp.int32, (rows, 1), 0) % tq) // CHUNK

    def step(cb, krb, kpos, first):
        n = cb.shape[0]
        kraw = _dot(cb, wkc_ref[...])
        ss_t = _dot_nt(mct_ref[...], (kraw * kraw).astype(BF16))
        sc_t = lax.rsqrt(ss_t + EPS)
        scale = jnp.concatenate([jnp.broadcast_to(sc_t[h:h + 1, :], (tq, n)) for h in range(MLA_HEADS)], axis=0)
        s = _dot_nt(qp, cb) * scale + _dot_nt(qr, krb)
        kchunk = (kpos + lax.broadcasted_iota(jnp.int32, (1, n), 1)) // CHUNK
        s = jnp.where(kchunk <= qchunk, s, NEG)
        mx = jnp.max(s, axis=-1, keepdims=True)
        if first:
            m_new = mx
        else:
            m_old = m_ref[...]
            m_new = jnp.maximum(m_old, mx)
            alpha = jnp.exp2(m_old - m_new)
        pr = jnp.exp2(s - m_new)
        sm = jnp.sum(pr, axis=-1, keepdims=True)
        pc = _dot(pr.astype(BF16), cb)
        m_ref[...] = m_new
        l_ref[...] = sm if first else alpha * l_ref[...] + sm
        acc_ref[...] = pc if first else alpha * acc_ref[...] + pc

    step(cn_ref[0].astype(BF16), krn_ref[0].astype(BF16), past, True)

    def body(t, carry):
        ks = pl.ds(pl.multiple_of(t * tk, tk), tk)
        step(cc_ref[0, ks, :].astype(BF16), ck_ref[0, ks, :].astype(BF16), t * tk, False)
        return carry

    lax.fori_loop(0, past // tk, body, 0)
    oc = (acc_ref[...] / l_ref[...]).astype(BF16)
    oall = _dot(oc, wv_ref[...])
    lane_head = lax.broadcasted_iota(jnp.int32, (tq, MLA_DIM), 1) // V_DIM
    o = jnp.zeros((tq, MLA_DIM), F32)
    for h in range(MLA_HEADS):
        o = jnp.where(lane_head == h, oall[h * tq:(h + 1) * tq], o)
    o = o * lax.rsqrt(jnp.mean(o * o, axis=-1, keepdims=True) + EPS) * g_ref[...]
    o_ref[0] = o.astype(o_ref.dtype)


def _attn_sample(q, c_new, kr_new, cache_c, cache_kr, wts, g, tk):
    b, t, _ = q.shape
    past = cache_c.shape[1]
    assert past % tk == 0
    rows = MLA_HEADS * t
    bspec =lambda shp: pl.BlockSpec((1,) + shp, lambda bi: (bi, 0, 0))
    qr = q.reshape(b, t, MLA_HEADS, HEAD_GROUP)[..., QK_NOPE:QK_NOPE + QK_ROPE]
    qr = jnp.swapaxes(qr, 1, 2).reshape(b, rows, QK_ROPE)
    return pl.pallas_call(
        functools.partial(_attn_sample_kernel, tk=tk, past=past), grid=(b,),
        in_specs=[bspec((t, QK_COLS)), bspec((rows, QK_ROPE)), bspec((t, KV_LORA)), bspec((t, QK_ROPE)),
                  bspec((past, KV_LORA)), bspec((past, QK_ROPE))]
                 + [_full_spec(w) for w in wts] + [_full_spec(g)],
        out_specs=bspec((t, MLA_DIM)),
        out_shape=jax.ShapeDtypeStruct((b, t, MLA_DIM), BF16),
        scratch_shapes=[pltpu.VMEM((rows, 1), F32), pltpu.VMEM((rows, 1), F32), pltpu.VMEM((rows, KV_LORA), F32)],
        compiler_params=_params(("arbitrary",)), name="attn_sample",
    )(q, qr, c_new, kr_new, cache_c, cache_kr, *wts, g)


def _tail_kernel(x_ref, at_ref, rw_ref, wo_ref, gf_ref, wg_ref, wu_ref, wd_ref, o_ref, *, fc):
    mix = jnp.concatenate([at_ref[...], rw_ref[...]], axis=1)
    h = x_ref[...] + _dot(mix, wo_ref[...])
    u = (h * lax.rsqrt(jnp.mean(h * h, axis=-1, keepdims=True) + EPS) * gf_ref[...]).astype(BF16)
    acc = h
    for j in range(D_FF // fc):
        cs = slice(j * fc, (j + 1) * fc)
        gt = _dot(u, wg_ref[:, cs])
        up = _dot(u, wu_ref[:, cs])
        act = (gt * jax.nn.sigmoid(gt) * up).astype(BF16)
        acc = acc + _dot(act, wd_ref[cs, :])
    o_ref[...] = acc


def _tail(x, attn, rw, wts, ts):
    n = x.shape[0]
    assert n % ts == 0
    tok = lambda w: pl.BlockSpec((ts, w), lambda i: (i, 0))
    wspec = lambda w: pl.BlockSpec(w.shape, lambda i: (0,) * w.ndim, pipeline_mode=pl.Buffered(1))
    return pl.pallas_call(
        functools.partial(_tail_kernel, fc=MXU_N), grid=(n // ts,),
        in_specs=[tok(D_MODEL), tok(MLA_DIM), tok(RWKV_DIM)] + [wspec(w) for w in wts],
        out_specs=tok(D_MODEL),
        out_shape=jax.ShapeDtypeStruct((n, D_MODEL), F32),
        compiler_params=_params(("arbitrary",)), name="tail",
    )(x, attn, rw, *wts)


def _block_matrix(size, seg_of, scale_of):
    seg = np.array([seg_of(i) for i in range(size)])
    m = np.zeros((size, size), np.float32)
    for i in range(size):
        if seg[i] >= 0:
            m[i, seg == seg[i]] = scale_of(i)
    return jnp.asarray(m, BF16)


def _rope_tables(pos):
    half = QK_ROPE // 2
    inv = ROPE_BASE ** (-jnp.arange(half, dtype=F32) / half)
    ang = pos.astype(F32)[:, None] * inv[None, :]
    cos, sin = jnp.cos(ang), jnp.sin(ang)
    t = pos.shape[0]
    cc = jnp.concatenate([cos, cos], axis=1)
    ss = jnp.concatenate([-sin, sin], axis=1)
    zq = jnp.zeros((t, LANE - QK_NOPE - QK_ROPE), F32)
    cosq = jnp.concatenate([jnp.ones((t, QK_NOPE), F32), cc, zq], axis=1)
    sinq = jnp.concatenate([jnp.zeros((t, QK_NOPE), F32), ss, zq], axis=1)
    zk = jnp.zeros((t, LANE - QK_ROPE), F32)
    return cosq, sinq, jnp.concatenate([cc, zk], axis=1), jnp.concatenate([ss, zk], axis=1)


def _prep_weights(norm_mix_g, w_in, q_norm_g, w_q_up, kv_norm_g, w_kv_up, qn_nope_g, qn_rope_g,
                  kn_nope_g, kn_rope_g, mu_shift, w0, w2, a0, a2, k_k, k_a):
    half = QK_ROPE // 2
    swap = np.concatenate([np.arange(half, QK_ROPE), np.arange(half)])
    row = lambda v: v.astype(F32).reshape(1, -1)
    pad_to = lambda v, n: jnp.concatenate([v, jnp.zeros(v.shape[:-1] + (n - v.shape[-1],), v.dtype)], axis=-1)

    mla_cols = Q_LORA + KV_LORA + QK_ROPE
    w_kr = w_in[:, Q_LORA + KV_LORA:mla_cols]
    win = jnp.concatenate([pad_to(jnp.concatenate([w_in[:, :mla_cols], w_kr[:, swap]], axis=1), MLA_EXT),
                           w_in[:, mla_cols:]], axis=1).astype(BF16)

    qh = w_q_up.reshape(Q_LORA, MLA_HEADS, QK_NOPE + QK_ROPE)
    zq = jnp.zeros((Q_LORA, MLA_HEADS, QK_NOPE), w_q_up.dtype)
    wqa = pad_to(qh, HEAD_GROUP).reshape(Q_LORA, QK_COLS).astype(BF16)
    wqb = pad_to(jnp.concatenate([zq, qh[:, :, QK_NOPE:][:, :, swap]], axis=2), HEAD_GROUP)
    wqb = wqb.reshape(Q_LORA, QK_COLS).astype(BF16)
    gqa = jnp.tile(pad_to(jnp.concatenate([qn_nope_g, qn_rope_g]), HEAD_GROUP), MLA_HEADS)
    gqb = jnp.tile(pad_to(jnp.concatenate([jnp.zeros_like(qn_nope_g), qn_rope_g[swap]]), HEAD_GROUP), MLA_HEADS)

    def seg_q(i):
        j = i % HEAD_GROUP
        base = (i // HEAD_GROUP) * 2
        return base if j < QK_NOPE else (base + 1 if j < QK_NOPE + QK_ROPE else -1)
    mq = _block_matrix(MXU_N, seg_q, lambda i: 1.0 / (QK_NOPE if i % HEAD_GROUP < QK_NOPE else QK_ROPE))
    mk = _block_matrix(MXU_N, lambda i: i // HEAD_GROUP if i % HEAD_GROUP < QK_NOPE else -1,
                       lambda i: 1.0 / QK_NOPE)
    m64sum = _block_matrix(MXU_N, lambda i: i // RWKV_HEAD, lambda i: 1.0)
    m64mean = _block_matrix(MXU_N, lambda i: i // RWKV_HEAD, lambda i: 1.0 / RWKV_HEAD)

    kvh = w_kv_up.reshape(KV_LORA, MLA_HEADS, QK_NOPE + V_DIM)
    wkk = pad_to(kvh[:, :, :QK_NOPE], HEAD_GROUP).reshape(KV_LORA, QK_COLS).astype(BF16)
    wv = kvh[:, :, QK_NOPE:].reshape(KV_LORA, MLA_DIM).astype(BF16)
    gkn = jnp.tile(pad_to(kn_nope_g, HEAD_GROUP), MLA_HEADS)
    erep_np = np.zeros((LANE, QK_COLS), np.float32)
    for h in range(MLA_HEADS):
        for j in range(QK_ROPE):
            erep_np[j, h * HEAD_GROUP + QK_NOPE + j] = 1.0
    erep = jnp.asarray(erep_np, BF16)
    gkr = pad_to(kn_rope_g, LANE)
    gkrs = pad_to(kn_rope_g[swap], LANE)

    w2e = jnp.concatenate([w2, jnp.zeros((AAA_LORA, RWKV_DIM), w2.dtype)], axis=0).astype(BF16)
    a2e = jnp.concatenate([jnp.zeros((DECAY_LORA, RWKV_DIM), a2.dtype), a2], axis=0).astype(BF16)

    inproj_w = [row(norm_mix_g), win, row(q_norm_g), wqa, wqb, row(gqa), row(gqb), mq,
                row(kv_norm_g), row(gkr), row(gkrs), wkk, row(gkn), mk, erep, wv, wv.T,
                row(mu_shift), row(w0), w2e, row(a0), a2e, row(k_k), row(k_a), m64sum]
    wkc = kvh[:, :, :QK_NOPE].reshape(KV_LORA, MLA_HEADS * QK_NOPE).astype(BF16)
    mct_np = np.zeros((16, MLA_HEADS * QK_NOPE), np.float32)
    for h in range(MLA_HEADS):
        mct_np[h, h * QK_NOPE:(h + 1) * QK_NOPE] = 1.0 / QK_NOPE
    sample_w = [wkc, row(kn_nope_g), jnp.asarray(mct_np, BF16), wv]
    return inproj_w, sample_w, m64mean, m64sum


def kernel(x_prompt, x_sample, cache_kv_latent, cache_k_rope, state_wkv, state_shift, meta_tokens, norm_mix_g, w_in, q_norm_g, w_q_up, kv_norm_g, w_kv_up, qn_nope_g, qn_rope_g, kn_nope_g, kn_rope_g, attn_out_g, mu_shift, w0, w2, a0, a2, g2, k_k, k_a, r_k, lnx_g, lnx_b, w_out, norm_ffn_g, w_gate, w_up, w_down):
    b, s, _ = x_prompt.shape
    db, ds, _ = x_sample.shape
    past = cache_kv_latent.shape[1]
    row = lambda v: v.astype(F32).reshape(1, -1)

    inproj_w, sample_w, m64mean, m64sum = _prep_weights(
        norm_mix_g, w_in, q_norm_g, w_q_up, kv_norm_g, w_kv_up, qn_nope_g, qn_rope_g,
        kn_nope_g, kn_rope_g, mu_shift, w0, w2, a0, a2, k_k, k_a)
    wkv_consts = (g2.astype(BF16), row(r_k), row(lnx_g), row(lnx_b), m64mean, m64sum)
    tail_w = [w_out.astype(BF16), row(norm_ffn_g), w_gate.astype(BF16), w_up.astype(BF16), w_down.astype(BF16)]
    g_attn = row(attn_out_g)

    ts = min(TOKEN_TILE, s)
    meta = _inproj(meta_tokens.astype(F32)[None], jnp.zeros((1, 1, RWKV_COLS), F32),
                   _rope_tables(jnp.arange(N_META)), inproj_w, N_META, values_transposed=True)
    c_m, kr_m, _, kc_m, vt_m = meta[:5]
    _, s_meta = _wkv(meta[5:12], jnp.zeros((1, RWKV_HEADS, RWKV_HEAD, RWKV_HEAD), F32),
                     wkv_consts, CHUNK, 1)

    pr = _inproj(x_prompt, meta[12], _rope_tables(N_META + jnp.arange(s)), inproj_w, ts, values_transposed=True,
                 lead_rows=(c_m[0], kr_m[0]))
    kv_latent_p, k_rope_p, q_p, kc_p, vt_p = pr[:5]
    rw_p, wkv_p = _wkv(pr[5:12], s_meta, wkv_consts, min(WKV_TILE, s), _streams(b, WKV_STREAMS))
    attn_p = _attn_prompt(q_p, kc_p, vt_p, kc_m[0], vt_m[0], g_attn, min(ATTN_TILE, s))
    y_prompt = _tail(x_prompt.reshape(b * s, D_MODEL), attn_p.reshape(b * s, MLA_DIM),
                     rw_p.reshape(b * s, RWKV_DIM), tail_w, ts).reshape(b, s, D_MODEL)

    sm = _inproj(x_sample, state_shift.astype(F32)[:, None, :], _rope_tables(past + jnp.arange(ds)),
                 inproj_w, ds, nb=_streams(db, SHORT_STREAMS))
    c_s, kr_s, q_s = sm[:3]
    rw_s, wkv_s = _wkv(sm[5:12], state_wkv.astype(F32), wkv_consts, CHUNK, _streams(db, SHORT_STREAMS))
    rw_s = rw_s[:, :ds]
    attn_s = _attn_sample(q_s, c_s, kr_s, cache_kv_latent, cache_k_rope, sample_w, g_attn,
                          min(SAMPLE_KEY_TILE, past))
    y_sample = _tail(x_sample.reshape(db * ds, D_MODEL), attn_s.reshape(db * ds, MLA_DIM),
                     rw_s.reshape(db * ds, RWKV_DIM), tail_w, min(TOKEN_TILE, db * ds)).reshape(db, ds, D_MODEL)

    return (y_prompt, y_sample, kv_latent_p, k_rope_p, wkv_p, pr[12][:, 0, :],
            c_s, kr_s, wkv_s, sm[12][:, 0, :])
```

```python
import functools

import numpy as np
import jax
import jax.numpy as jnp
from jax import lax
from jax.experimental import pallas as pl
from jax.experimental.pallas import tpu as pltpu

F32 = jnp.float32
BF16 = jnp.bfloat16

D_MODEL = 1024
CHUNK = 64
N_META = 16
EPS = 1e-6
NEG = -1e30
MLA_HEADS = 8
QK_NOPE = 64
QK_ROPE = 32
V_DIM = 64
Q_LORA = 256
KV_LORA = 128
ROPE_BASE = 10000.0
ATTN_SCALE = (QK_NOPE + QK_ROPE) ** -0.5
LOG2E = float(np.log2(np.e))
MLA_DIM = MLA_HEADS * V_DIM
RWKV_HEADS = 8
RWKV_HEAD = 64
RWKV_DIM = RWKV_HEADS * RWKV_HEAD
DECAY_LORA = 64
AAA_LORA = 64
GATE_LORA = 128
LNX_EPS = 64e-5
RWKV_COLS = 3 * RWKV_DIM + DECAY_LORA + AAA_LORA + GATE_LORA
D_FF = -(-8 * D_MODEL // (3 * 256)) * 256

LANE = 128
MXU_N = 256
HEAD_GROUP = LANE
QK_COLS = MLA_HEADS * HEAD_GROUP
MLA_EXT = 512
IN_EXT = MLA_EXT + RWKV_COLS
VMEM_LIMIT = 56 * 1024 * 1024

TOKEN_TILE = 512
ATTN_TILE = 512
SAMPLE_KEY_TILE = 4096
WKV_TILE = 4 * CHUNK
WKV_STREAMS = 2
SHORT_STREAMS = 4


def _streams(batch, want):
    return want if batch % want == 0 else 1


def _dot(a, b):
    return jnp.dot(a, b, preferred_element_type=F32)


def _dot_nt(a, b):
    return lax.dot_general(a, b, (((1,), (1,)), ((), ())), preferred_element_type=F32)


def _dot_tn(a, b):
    return lax.dot_general(a, b, (((0,), (0,)), ((), ())), preferred_element_type=F32)


def _seg_reduce(x, m, terms=2):
    parts = [x.astype(BF16)]
    if terms == 2:
        parts.append((x - parts[0].astype(F32)).astype(BF16))
    outs = []
    for j in range(x.shape[1] // MXU_N):
        sl = slice(j * MXU_N, (j + 1) * MXU_N)
        outs.append(sum(_dot(part[:, sl], m) for part in parts))
    return outs[0] if len(outs) == 1 else jnp.concatenate(outs, axis=1)


def _full_spec(arr):
    nd = arr.ndim
    return pl.BlockSpec(arr.shape, lambda *_: (0,) * nd)


def _params(sem):
    return pltpu.CompilerParams(dimension_semantics=sem, vmem_limit_bytes=VMEM_LIMIT)


def _inproj_kernel(x_ref, prev0_ref, cosq_ref, sinq_ref, cosk_ref, sink_ref,
                   gmix_ref, win_ref, gql_ref, wqa_ref, wqb_ref, gqa_ref, gqb_ref, mq_ref,
                   gkv_ref, gkr_ref, gkrs_ref, wkk_ref, gkn_ref, mk_ref, erep_ref, wv_ref, wvt_ref,
                   mu_ref, w0_ref, w2_ref, a0_ref, a2_ref, kkg_ref, ka_ref, m64_ref,
                   cm_ref, krm_ref,
                   c_ref, kr_ref, q_ref, kc_ref, vc_ref,
                   r_ref, lw_ref, k_ref, v_ref, kk_ref, b_ref, gl_ref, shift_ref,
                   carry_ref, *, values_transposed, lead):
    i = pl.program_id(1)
    nb, ts = x_ref.shape[:2]
    part = lambda val, bi: val[bi * ts:(bi + 1) * ts]

    def put(ref, val):
        for bi in range(nb):
            ref[bi] = part(val, bi)

    def put_rows(ref, val):
        if lead:
            ref[0, pl.ds(pl.multiple_of(lead + i * ts, 8), ts), :] = val
        else:
            put(ref, val)

    def put_padded(ref, val):
        if ref.shape[1] == ts:
            put(ref, val)
        else:
            for bi in range(nb):
                ref[bi, :ts, :] = part(val, bi)
                ref[bi, ts:, :] = jnp.zeros((ref.shape[1] - ts, val.shape[1]), val.dtype)

    if lead:
        @pl.when(i == 0)
        def _():
            c_ref[0, :lead, :] = cm_ref[...]
            kr_ref[0, :lead, :] = krm_ref[...]

    x = x_ref[0] if nb == 1 else jnp.concatenate([x_ref[bi] for bi in range(nb)], axis=0)
    tab = lambda ref: ref[...] if nb == 1 else jnp.concatenate([ref[...]] * nb, axis=0)
    xn = x * lax.rsqrt(jnp.mean(x * x, axis=-1, keepdims=True) + EPS) * gmix_ref[...]
    p = _dot(xn.astype(BF16), win_ref[...])

    pq = p[:, :Q_LORA]
    ql = pq * lax.rsqrt(jnp.mean(pq * pq, axis=-1, keepdims=True) + EPS) * gql_ref[...]
    ql = ql.astype(BF16)
    qa = _dot(ql, wqa_ref[...])
    qb = _dot(ql, wqb_ref[...])
    qs = lax.rsqrt(_seg_reduce(qa * qa, mq_ref[...], terms=1) + EPS)
    fa = tab(cosq_ref) * (gqa_ref[...] * (ATTN_SCALE * LOG2E))
    fb = tab(sinq_ref) * (gqb_ref[...] * (ATTN_SCALE * LOG2E))
    q = qs * (qa * jnp.concatenate([fa] * MLA_HEADS, axis=1) + qb * jnp.concatenate([fb] * MLA_HEADS, axis=1))
    put(q_ref, q.astype(BF16))

    pc = p[:, Q_LORA:Q_LORA + KV_LORA]
    c = pc * lax.rsqrt(jnp.mean(pc * pc, axis=-1, keepdims=True) + EPS) * gkv_ref[...]
    put_rows(c_ref, c)
    cb = c.astype(BF16)

    pk = p[:, Q_LORA + KV_LORA:MLA_EXT]
    lane = lax.broadcasted_iota(jnp.int32, pk.shape, 1)
    ssk = jnp.sum(jnp.where(lane < QK_ROPE, pk * pk, 0.0), axis=-1, keepdims=True)
    sk = lax.rsqrt(ssk * (1.0 / QK_ROPE) + EPS)
    pk_sw = pltpu.roll(pk, LANE - QK_ROPE, axis=1)
    kr = sk * (pk * gkr_ref[...] * tab(cosk_ref) + pk_sw * gkrs_ref[...] * tab(sink_ref))
    put_rows(kr_ref, kr[:, :QK_ROPE])

    kraw = _dot(cb, wkk_ref[...])
    kn = kraw * lax.rsqrt(_seg_reduce(kraw * kraw, mk_ref[...], terms=1) + EPS) * gkn_ref[...]
    put(kc_ref, (kn + _dot(kr.astype(BF16), erep_ref[...])).astype(BF16))
    if values_transposed:
        vc_ref[0] = _dot_nt(wvt_ref[...], cb).astype(BF16)
    else:
        put(vc_ref, _dot(cb, wv_ref[...]).astype(BF16))

    prw = p[:, MLA_EXT:]

    @pl.when(i == 0)
    def _():
        carry_ref[...] = jnp.broadcast_to(prev0_ref[...], carry_ref.shape)

    row = lax.broadcasted_iota(jnp.int32, (nb * ts, 1), 0)
    shifted = pltpu.roll(prw, 1, axis=0)
    for bi in range(nb):
        shifted = jnp.where(row == bi * ts, carry_ref[bi], shifted)
        last = prw[(bi + 1) * ts - 1:(bi + 1) * ts, :]
        carry_ref[bi] = last
        shift_ref[bi] = last
    xm = prw + (shifted - prw) * mu_ref[...]

    o = RWKV_DIM
    r = xm[:, :o]
    k = xm[:, o:2 * o]
    v = xm[:, 2 * o:3 * o]
    wa = xm[:, 3 * o:3 * o + DECAY_LORA + AAA_LORA]
    put_padded(gl_ref, xm[:, 3 * o + DECAY_LORA + AAA_LORA:])
    dw = _dot(jnp.tanh(wa).astype(BF16), w2_ref[...])
    da = _dot(wa.astype(BF16), a2_ref[...])
    put_padded(lw_ref, -jax.nn.sigmoid(w0_ref[...] + dw) * float(np.exp(-0.5)))
    a = jax.nn.sigmoid(a0_ref[...] + da)
    kx = k * kkg_ref[...]
    kk = kx * lax.rsqrt(jnp.maximum(_seg_reduce(kx * kx, m64_ref[...], terms=1), 1e-24))
    put_padded(r_ref, r)
    put_padded(k_ref, k * (1.0 + (a - 1.0) * ka_ref[...]))
    put_padded(v_ref, v)
    put_padded(kk_ref, kk)
    put_padded(b_ref, kk * a)


def _inproj(x, prev0, tabs, wts, ts, values_transposed=False, lead_rows=None, nb=1):
    b, t, _ = x.shape
    assert t % ts == 0 and b % nb == 0
    assert nb == 1 or not (values_transposed or lead_rows is not None)
    grid = (b // nb, t // ts)
    tok = lambda w: pl.BlockSpec((nb, ts, w), lambda bi, i: (bi, i, 0))
    lead = 0 if lead_rows is None else lead_rows[0].shape[0]
    if lead_rows is None:
        lead_rows = (jnp.zeros((8, KV_LORA), F32), jnp.zeros((8, QK_ROPE), F32))
    tpad = t if t % CHUNK == 0 else CHUNK
    assert tpad == t or t == ts
    in_specs = [tok(D_MODEL),
                pl.BlockSpec((nb if prev0.shape[0] > 1 else 1, 1, RWKV_COLS),
                             (lambda bi, i: (bi, 0, 0)) if prev0.shape[0] > 1 else (lambda bi, i: (0, 0, 0)))]
    in_specs += [pl.BlockSpec((ts, LANE), lambda bi, i: (i, 0)) for _ in tabs]
    in_specs += [_full_spec(w) for w in wts] + [_full_spec(w) for w in lead_rows]
    widths = [(KV_LORA, F32), (QK_ROPE, F32), (QK_COLS, BF16), (QK_COLS, BF16), (MLA_DIM, BF16)]
    widths += [(RWKV_DIM, F32)] * 6 + [(GATE_LORA, F32)]
    out_shape = [jax.ShapeDtypeStruct((b, t, w), dt) for w, dt in widths]
    out_specs = [tok(w) for w, _ in widths]
    if values_transposed:
        out_shape[4] = jax.ShapeDtypeStruct((b, MLA_DIM, t), BF16)
        out_specs[4] = pl.BlockSpec((1, MLA_DIM, ts), lambda bi, i: (bi, 0, i))
    if lead:
        for n in range(2):
            w, dt = widths[n]
            out_shape[n] = jax.ShapeDtypeStruct((b, lead + t, w), dt)
            out_specs[n] = pl.BlockSpec((1, lead + t, w), lambda bi, i: (bi, 0, 0))
    if tpad != t:
        for n in range(5, 12):
            w, dt = widths[n]
            out_shape[n] = jax.ShapeDtypeStruct((b, tpad, w), dt)
            out_specs[n] = pl.BlockSpec((nb, tpad, w), lambda bi, i: (bi, 0, 0))
    out_shape.append(jax.ShapeDtypeStruct((b, 1, RWKV_COLS), F32))
    out_specs.append(pl.BlockSpec((nb, 1, RWKV_COLS), lambda bi, i: (bi, 0, 0)))
    return pl.pallas_call(
        functools.partial(_inproj_kernel, values_transposed=values_transposed, lead=lead), grid=grid, in_specs=in_specs, out_specs=out_specs, out_shape=out_shape,
        scratch_shapes=[pltpu.VMEM((nb, 1, RWKV_COLS), F32)],
        compiler_params=_params(("arbitrary", "arbitrary")), name="inproj",
    )(x, prev0, *tabs, *wts, *lead_rows)


PAIRS = RWKV_HEADS // 2
PAIR_W = 2 * RWKV_HEAD


def _wkv_kernel(r_ref, lw_ref, k_ref, v_ref, kk_ref, b_ref, gl_ref, s0_ref,
                tri_ref, g2_ref, rk_ref, lng_ref, lnb_ref, m64mean_ref, m64sum_ref,
                rw_ref, sout_ref, state_ref):
    i = pl.program_id(1)
    nb, tb = r_ref.shape[:2]
    tt = nb * tb
    c = CHUNK
    nch = tt // c
    cps = tb // c

    @pl.when(i == 0)
    def _():
        state_ref[...] = jnp.broadcast_to(s0_ref[...], state_ref.shape)

    rows_of = lambda ref: ref[0] if nb == 1 else jnp.concatenate([ref[bi] for bi in range(nb)], axis=0)
    r = rows_of(r_ref)
    lw = rows_of(lw_ref)
    k = rows_of(k_ref)
    v = rows_of(v_ref)
    tri = tri_ref[...]
    l1 = lw.astype(BF16)
    e1 = lw - l1.astype(F32)
    l2 = e1.astype(BF16)
    l3 = (e1 - l2.astype(F32)).astype(BF16)
    psum = lambda x: _dot(tri, x[:tb]) if nb == 1 else jnp.concatenate(
        [_dot(tri, x[bi * tb:(bi + 1) * tb]) for bi in range(nb)], axis=0)
    cum = psum(l1) + psum(l2) + psum(l3)
    g = jnp.exp(cum)
    gi = jnp.exp(-cum)
    gp = jnp.exp(cum - lw)
    gc_rows = [g[j * c + c - 1:(j + 1) * c, :] for j in range(nch)]
    gcb = jnp.concatenate([jnp.broadcast_to(x, (c, RWKV_DIM)) for x in gc_rows], axis=0)
    at = -(rows_of(kk_ref) * gp)
    bt = rows_of(b_ref) * gi
    kt = k * gi
    rt = r * g

    even = (lax.broadcasted_iota(jnp.int32, (tt, RWKV_DIM), 1) % PAIR_W) < RWKV_HEAD
    split = lambda x: (jnp.where(even, x, 0.0).astype(BF16), jnp.where(even, 0.0, x).astype(BF16))
    atm_b, rtm_b, vm_b = split(at), split(rt), split(v)
    bt_b, kt_b, v_b = bt.astype(BF16), kt.astype(BF16), v.astype(BF16)
    btc_b, ktc_b = (bt * gcb).astype(BF16), (kt * gcb).astype(BF16)

    lane = lax.broadcasted_iota(jnp.int32, (c, PAIR_W), 1)
    lo = lane < c
    eye_hi = (lane == lax.broadcasted_iota(jnp.int32, (c, PAIR_W), 0) + c).astype(F32)
    rowi = lax.broadcasted_iota(jnp.int32, (2 * c, 2 * c), 0)
    coli = lax.broadcasted_iota(jnp.int32, (2 * c, 2 * c), 1)
    colt = jnp.where(coli >= c, coli - c, coli)
    gmask = colt < jnp.where(rowi < c, rowi, rowi - c + 1)
    blk = (rowi // c) == (coli // c)
    z1 = jnp.zeros((c, PAIR_W), BF16)
    z2 = jnp.zeros((c, 2 * PAIR_W), BF16)

    items = [(j, h) for j in range(nch) for h in range(RWKV_HEADS)]
    rows = lambda j: slice(j * c, (j + 1) * c)
    lanes = lambda h: slice((h // 2) * PAIR_W, (h // 2 + 1) * PAIR_W)
    cut = lambda x, j, h: x[rows(j), lanes(h)]
    vstack = lambda a, b: jnp.concatenate([a, b], axis=0)
    hstack = lambda a, b: jnp.concatenate([a, b], axis=1)

    gb, xs = [], []
    for j, h in items:
        ar = vstack(cut(atm_b[h % 2], j, h), cut(rtm_b[h % 2], j, h))
        bk = vstack(cut(bt_b, j, h), cut(kt_b, j, h))
        gm = jnp.where(gmask, _dot_nt(ar, bk), 0.0)
        gb.append(gm.astype(BF16))
        xs.append(jnp.where(lo, gm[:c], eye_hi))
    for _ in range(int(np.log2(c)) - 1):
        xs = [_dot(x[:, :c].astype(BF16), x.astype(BF16)) + jnp.where(lo, 0.0, x) for x in xs]
    xb = [x.astype(BF16) for x in xs]
    vm = [cut(vm_b[h % 2], j, h) for j, h in items]
    akv = [_dot(gb[n][:c], vstack(z1, vm[n])) for n in range(len(items))]
    zc = [hstack(cut(atm_b[h % 2], j, h), akv[n].astype(BF16)) for n, (j, h) in enumerate(items)]
    w = [_dot(xb[n], vstack(z2, zc[n])) for n in range(len(items))]
    tu = [w[n] + _dot(xb[n], vstack(w[n].astype(BF16), z2)) for n in range(len(items))]
    ry = [_dot(gb[n][c:], vstack(tu[n].astype(BF16), hstack(z1, vm[n]))) for n in range(len(items))]

    reff, yv, pm, qm = {}, {}, {}, {}
    for j in range(nch):
        for p in range(PAIRS):
            n0 = j * RWKV_HEADS + 2 * p
            tu_p = tu[n0] + tu[n0 + 1]
            ry_p = ry[n0] + ry[n0 + 1]
            h = 2 * p
            reff[j, p] = (cut(rt, j, h) + ry_p[:, :PAIR_W]).astype(BF16)
            yv[j, p] = ry_p[:, PAIR_W:]
            pm[j, p] = jnp.where(blk, _dot_tn(tu_p[:, :PAIR_W].astype(BF16), cut(btc_b, j, h)), 0.0).astype(BF16)
            qm[j, p] = jnp.where(blk, _dot_tn(vstack(tu_p[:, PAIR_W:].astype(BF16), cut(v_b, j, h)),
                                              vstack(cut(btc_b, j, h), cut(ktc_b, j, h))), 0.0)

    st = [[state_ref[bi, p] for p in range(PAIRS)] for bi in range(nb)]
    ys = [None] * nch
    for jj in range(cps):
        for bi in range(nb):
            j = bi * cps + jj
            sb = [s.astype(BF16) for s in st[bi]]
            ys[j] = jnp.concatenate([_dot_nt(reff[j, p], sb[p]) + yv[j, p] for p in range(PAIRS)], axis=1)
            st[bi] = [st[bi][p] * gc_rows[j][:, p * PAIR_W:(p + 1) * PAIR_W] + _dot(sb[p], pm[j, p]) + qm[j, p]
                      for p in range(PAIRS)]
    for bi in range(nb):
        for p in range(PAIRS):
            state_ref[bi, p] = st[bi][p]
    sout_ref[...] = state_ref[...]

    y = ys[0] if nch == 1 else jnp.concatenate(ys, axis=0)
    mean = _seg_reduce(y, m64mean_ref[...])
    d = y - mean
    var = _seg_reduce(d * d, m64mean_ref[...], terms=1)
    yn = d * lax.rsqrt(var + LNX_EPS) * lng_ref[...] + lnb_ref[...]
    bonus = _seg_reduce(r * k * rk_ref[...], m64sum_ref[...], terms=1) * v
    gate = _dot(jax.nn.sigmoid(rows_of(gl_ref)).astype(BF16), g2_ref[...])
    rw = ((yn + bonus) * gate).astype(rw_ref.dtype)
    for bi in range(nb):
        rw_ref[bi] = rw[bi * tb:(bi + 1) * tb]


def _state_to_pairs(s):
    z = jnp.zeros_like(s[:, 0::2])
    top = jnp.concatenate([s[:, 0::2], z], axis=-1)
    bot = jnp.concatenate([z, s[:, 1::2]], axis=-1)
    return jnp.concatenate([top, bot], axis=2)


def _pairs_to_state(sp):
    b = sp.shape[0]
    s = jnp.stack([sp[:, :, :RWKV_HEAD, :RWKV_HEAD], sp[:, :, RWKV_HEAD:, RWKV_HEAD:]], axis=2)
    return s.reshape(b, RWKV_HEADS, RWKV_HEAD, RWKV_HEAD)


def _wkv(seqs, s0, consts, tt, nb):
    b, t, _ = seqs[0].shape
    assert b % nb == 0 and t % tt == 0 and tt % CHUNK == 0
    tok = lambda w: pl.BlockSpec((nb, tt, w), lambda bi, i: (bi, i, 0))
    st_shape = (nb, PAIRS, PAIR_W, PAIR_W)
    in_specs = [tok(RWKV_DIM)] * 6 + [tok(GATE_LORA)]
    s0_shape = (1,) + st_shape[1:] if s0.shape[0] == 1 else st_shape
    in_specs.append(pl.BlockSpec(s0_shape, (lambda bi, i: (bi, 0, 0, 0)) if s0.shape[0] > 1 else (lambda bi, i: (0, 0, 0, 0))))
    tri = jnp.kron(jnp.eye(tt // CHUNK, dtype=F32), jnp.tril(jnp.ones((CHUNK, CHUNK), F32))).astype(BF16)
    consts = (tri,) + tuple(consts)
    in_specs += [_full_spec(w) for w in consts]
    rw, sp = pl.pallas_call(
        _wkv_kernel, grid=(b // nb, t // tt),
        in_specs=in_specs,
        out_specs=[tok(RWKV_DIM), pl.BlockSpec(st_shape, lambda bi, i: (bi, 0, 0, 0))],
        out_shape=[jax.ShapeDtypeStruct((b, t, RWKV_DIM), BF16),
                   jax.ShapeDtypeStruct((b,) + st_shape[1:], F32)],
        scratch_shapes=[pltpu.VMEM(st_shape, F32)],
        compiler_params=_params(("arbitrary", "arbitrary")), name="wkv",
    )(*seqs, _state_to_pairs(s0), *consts)
    return rw, _pairs_to_state(sp)


_hq = lambda h: slice(h * HEAD_GROUP, (h + 1) * HEAD_GROUP)


L_ROWS = 16
ACC_ROWS = V_DIM + L_ROWS


def _attn_step_t(qh, kt, vt, mask, first, m_ref, acc_ref, qw=slice(None)):
    n = vt(0).shape[1]
    ones = jnp.ones((L_ROWS, n), BF16)
    s = [_dot_nt(kt(h), qh(h)) for h in range(MLA_HEADS)]
    if mask is not None:
        s = [jnp.where(mask, x, NEG) for x in s]
    ps, alphas = [], []
    for h in range(MLA_HEADS):
        mx = jnp.max(s[h], axis=0, keepdims=True)
        if first:
            m_new = mx
        else:
            m_old = m_ref[h, :, qw]
            m_new = jnp.maximum(m_old, mx)
            alphas.append(jnp.exp2(m_old - m_new))
        m_ref[h, :, qw] = m_new
        ps.append(jnp.exp2(s[h] - m_new).astype(BF16))
    for h in range(MLA_HEADS):
        pv = _dot(jnp.concatenate([vt(h), ones], axis=0), ps[h])
        acc_ref[h, :, qw] = pv if first else alphas[h] * acc_ref[h, :, qw] + pv


def _attn_prompt_kernel(q_ref, k_ref, vt_ref, km_ref, vtm_ref, g_ref, o_ref, m_ref, acc_ref, *, tk):
    i = pl.program_id(1)
    tq = q_ref.shape[1]
    assert tq == tk and tq % (2 * CHUNK) == 0
    half = tq // 2
    st = (m_ref, acc_ref)
    qh = lambda h: q_ref[0, :, _hq(h)]
    hd = lambda h: slice(h * V_DIM, (h + 1) * V_DIM)

    _attn_step_t(qh, lambda h: km_ref[:, _hq(h)], lambda h: vtm_ref[hd(h), :], None, True, *st)

    def body(t, carry):
        ks = pl.ds(pl.multiple_of(t * tk, tk), tk)
        _attn_step_t(qh, lambda h: k_ref[0, ks, _hq(h)], lambda h: vt_ref[0, hd(h), ks], None, False, *st)
        return carry

    lax.fori_loop(0, i, body, 0)
    k0 = pl.ds(pl.multiple_of(i * tk, tk), half)
    k1 = pl.ds(pl.multiple_of(i * tk + half, half), half)
    krow = lax.broadcasted_iota(jnp.int32, (half, tq), 0) // CHUNK
    qcol = lax.broadcasted_iota(jnp.int32, (half, tq), 1) // CHUNK
    _attn_step_t(qh, lambda h: k_ref[0, k0, _hq(h)], lambda h: vt_ref[0, hd(h), k0], krow <= qcol, False, *st)
    _attn_step_t(lambda h: q_ref[0, half:, _hq(h)], lambda h: k_ref[0, k1, _hq(h)],
                 lambda h: vt_ref[0, hd(h), k1], (krow <= qcol)[:, :half], False, *st, qw=slice(half, tq))

    o_t = jnp.concatenate([acc_ref[h, :V_DIM, :] / acc_ref[h, V_DIM:V_DIM + 1, :] for h in range(MLA_HEADS)],
                          axis=0)
    o_t = o_t * lax.rsqrt(jnp.mean(o_t * o_t, axis=0, keepdims=True) + EPS)
    o_ref[0] = (o_t.T * g_ref[...]).astype(o_ref.dtype)


def _attn_prompt(q, kc, vt, kmeta, vtmeta, g, tq):
    b, s, _ = q.shape
    assert s % tq == 0
    return pl.pallas_call(
        functools.partial(_attn_prompt_kernel, tk=tq), grid=(b, s // tq),
        in_specs=[pl.BlockSpec((1, tq, QK_COLS), lambda bi, i: (bi, i, 0)),
                  pl.BlockSpec((1, s, QK_COLS), lambda bi, i: (bi, 0, 0)),
                  pl.BlockSpec((1, MLA_DIM, s), lambda bi, i: (bi, 0, 0)),
                  _full_spec(kmeta), _full_spec(vtmeta), _full_spec(g)],
        out_specs=pl.BlockSpec((1, tq, MLA_DIM), lambda bi, i: (bi, i, 0)),
        out_shape=jax.ShapeDtypeStruct((b, s, MLA_DIM), BF16),
        scratch_shapes=[pltpu.VMEM((MLA_HEADS, 1, tq), F32), pltpu.VMEM((MLA_HEADS, ACC_ROWS, tq), F32)],
        compiler_params=_params(("arbitrary", "arbitrary")), name="attn_prompt",
    )(q, kc, vt, kmeta, vtmeta, g)


def _attn_sample_kernel(q_ref, qr_ref, cn_ref, krn_ref, cc_ref, ck_ref, wkc_ref, gk_ref, mct_ref, wv_ref,
                        g_ref, o_ref, m_ref, l_ref, acc_ref, *, tk, past):
    tq = q_ref.shape[1]
    rows = MLA_HEADS * tq
    qp = jnp.concatenate(
        [_dot_nt((q_ref[0, :, h * HEAD_GROUP:h * HEAD_GROUP + QK_NOPE].astype(F32) * gk_ref[...]).astype(BF16),
                 wkc_ref[:, h * QK_NOPE:(h + 1) * QK_NOPE]) for h in range(MLA_HEADS)], axis=0).astype(BF16)
    qr = qr_ref[0]
    qchunk = (past + lax.broadcasted_iota(jnp.int32, (rows, 1), 0) % tq) // CHUNK

    def step(cb, krb, kpos, first):
        n = cb.shape[0]
        kraw = _dot(cb, wkc_ref[...])
        ss_t = _dot_nt(mct_ref[...], (kraw * kraw).astype(BF16))
        sc_t = lax.rsqrt(ss_t + EPS)
        scale = jnp.concatenate([jnp.broadcast_to(sc_t[h:h + 1, :], (tq, n)) for h in range(MLA_HEADS)], axis=0)
        s = _dot_nt(qp, cb) * scale + _dot_nt(qr, krb)
        kchunk = (kpos + lax.broadcasted_iota(jnp.int32, (1, n), 1)) // CHUNK
        s = jnp.where(kchunk <= qchunk, s, NEG)
        mx = jnp.max(s, axis=-1, keepdims=True)
        if first:
            m_new = mx
        else:
            m_old = m_ref[...]
            m_new = jnp.maximum(m_old, mx)
            alpha = jnp.exp2(m_old - m_new)
        pr = jnp.exp2(s - m_new)
        sm = jnp.sum(pr, axis=-1, keepdims=True)
        pc = _dot(pr.astype(BF16), cb)
        m_ref[...] = m_new
        l_ref[...] = sm if first else alpha * l_ref[...] + sm
        acc_ref[...] = pc if first else alpha * acc_ref[...] + pc

    step(cn_ref[0].astype(BF16), krn_ref[0].astype(BF16), past, True)

    def body(t, carry):
        ks = pl.ds(pl.multiple_of(t * tk, tk), tk)
        step(cc_ref[0, ks, :].astype(BF16), ck_ref[0, ks, :].astype(BF16), t * tk, False)
        return carry

    lax.fori_loop(0, past // tk, body, 0)
    oc = (acc_ref[...] / l_ref[...]).astype(BF16)
    oall = _dot(oc, wv_ref[...])
    lane_head = lax.broadcasted_iota(jnp.int32, (tq, MLA_DIM), 1) // V_DIM
    o = jnp.zeros((tq, MLA_DIM), F32)
    for h in range(MLA_HEADS):
        o = jnp.where(lane_head == h, oall[h * tq:(h + 1) * tq], o)
    o = o * lax.rsqrt(jnp.mean(o * o, axis=-1, keepdims=True) + EPS) * g_ref[...]
    o_ref[0] = o.astype(o_ref.dtype)


def _attn_sample(q, c_new, kr_new, cache_c, cache_kr, wts, g, tk):
    b, t, _ = q.shape
    past = cache_c.shape[1]
    assert past % tk == 0
    rows = MLA_HEADS * t
    bspec =lambda shp: pl.BlockSpec((1,) + shp, lambda bi: (bi, 0, 0))
    qr = q.reshape(b, t, MLA_HEADS, HEAD_GROUP)[..., QK_NOPE:QK_NOPE + QK_ROPE]
    qr = jnp.swapaxes(qr, 1, 2).reshape(b, rows, QK_ROPE)
    return pl.pallas_call(
        functools.partial(_attn_sample_kernel, tk=tk, past=past), grid=(b,),
        in_specs=[bspec((t, QK_COLS)), bspec((rows, QK_ROPE)), bspec((t, KV_LORA)), bspec((t, QK_ROPE)),
                  bspec((past, KV_LORA)), bspec((past, QK_ROPE))]
                 + [_full_spec(w) for w in wts] + [_full_spec(g)],
        out_specs=bspec((t, MLA_DIM)),
        out_shape=jax.ShapeDtypeStruct((b, t, MLA_DIM), BF16),
        scratch_shapes=[pltpu.VMEM((rows, 1), F32), pltpu.VMEM((rows, 1), F32), pltpu.VMEM((rows, KV_LORA), F32)],
        compiler_params=_params(("arbitrary",)), name="attn_sample",
    )(q, qr, c_new, kr_new, cache_c, cache_kr, *wts, g)


def _tail_kernel(x_ref, at_ref, rw_ref, wo_ref, gf_ref, wg_ref, wu_ref, wd_ref, o_ref, *, fc):
    mix = jnp.concatenate([at_ref[...], rw_ref[...]], axis=1)
    h = x_ref[...] + _dot(mix, wo_ref[...])
    u = (h * lax.rsqrt(jnp.mean(h * h, axis=-1, keepdims=True) + EPS) * gf_ref[...]).astype(BF16)
    acc = h
    for j in range(D_FF // fc):
        cs = slice(j * fc, (j + 1) * fc)
        gt = _dot(u, wg_ref[:, cs])
        up = _dot(u, wu_ref[:, cs])
        act = (gt * jax.nn.sigmoid(gt) * up).astype(BF16)
        acc = acc + _dot(act, wd_ref[cs, :])
    o_ref[...] = acc


def _tail(x, attn, rw, wts, ts):
    n = x.shape[0]
    assert n % ts == 0
    tok = lambda w: pl.BlockSpec((ts, w), lambda i: (i, 0))
    wspec = lambda w: pl.BlockSpec(w.shape, lambda i: (0,) * w.ndim, pipeline_mode=pl.Buffered(1))
    return pl.pallas_call(
        functools.partial(_tail_kernel, fc=MXU_N), grid=(n // ts,),
        in_specs=[tok(D_MODEL), tok(MLA_DIM), tok(RWKV_DIM)] + [wspec(w) for w in wts],
        out_specs=tok(D_MODEL),
        out_shape=jax.ShapeDtypeStruct((n, D_MODEL), F32),
        compiler_params=_params(("arbitrary",)), name="tail",
    )(x, attn, rw, *wts)


def _block_matrix(size, seg_of, scale_of):
    seg = np.array([seg_of(i) for i in range(size)])
    m = np.zeros((size, size), np.float32)
    for i in range(size):
        if seg[i] >= 0:
            m[i, seg == seg[i]] = scale_of(i)
    return jnp.asarray(m, BF16)


def _rope_tables(pos):
    half = QK_ROPE // 2
    inv = ROPE_BASE ** (-jnp.arange(half, dtype=F32) / half)
    ang = pos.astype(F32)[:, None] * inv[None, :]
    cos, sin = jnp.cos(ang), jnp.sin(ang)
    t = pos.shape[0]
    cc = jnp.concatenate([cos, cos], axis=1)
    ss = jnp.concatenate([-sin, sin], axis=1)
    zq = jnp.zeros((t, LANE - QK_NOPE - QK_ROPE), F32)
    cosq = jnp.concatenate([jnp.ones((t, QK_NOPE), F32), cc, zq], axis=1)
    sinq = jnp.concatenate([jnp.zeros((t, QK_NOPE), F32), ss, zq], axis=1)
    zk = jnp.zeros((t, LANE - QK_ROPE), F32)
    return cosq, sinq, jnp.concatenate([cc, zk], axis=1), jnp.concatenate([ss, zk], axis=1)


def _prep_weights(norm_mix_g, w_in, q_norm_g, w_q_up, kv_norm_g, w_kv_up, qn_nope_g, qn_rope_g,
                  kn_nope_g, kn_rope_g, mu_shift, w0, w2, a0, a2, k_k, k_a):
    half = QK_ROPE // 2
    swap = np.concatenate([np.arange(half, QK_ROPE), np.arange(half)])
    row = lambda v: v.astype(F32).reshape(1, -1)
    pad_to = lambda v, n: jnp.concatenate([v, jnp.zeros(v.shape[:-1] + (n - v.shape[-1],), v.dtype)], axis=-1)

    mla_cols = Q_LORA + KV_LORA + QK_ROPE
    w_kr = w_in[:, Q_LORA + KV_LORA:mla_cols]
    win = jnp.concatenate([pad_to(jnp.concatenate([w_in[:, :mla_cols], w_kr[:, swap]], axis=1), MLA_EXT),
                           w_in[:, mla_cols:]], axis=1).astype(BF16)

    qh = w_q_up.reshape(Q_LORA, MLA_HEADS, QK_NOPE + QK_ROPE)
    zq = jnp.zeros((Q_LORA, MLA_HEADS, QK_NOPE), w_q_up.dtype)
    wqa = pad_to(qh, HEAD_GROUP).reshape(Q_LORA, QK_COLS).astype(BF16)
    wqb = pad_to(jnp.concatenate([zq, qh[:, :, QK_NOPE:][:, :, swap]], axis=2), HEAD_GROUP)
    wqb = wqb.reshape(Q_LORA, QK_COLS).astype(BF16)
    gqa = pad_to(jnp.concatenate([qn_nope_g, qn_rope_g]), HEAD_GROUP)
    gqb = pad_to(jnp.concatenate([jnp.zeros_like(qn_nope_g), qn_rope_g[swap]]), HEAD_GROUP)

    def seg_q(i):
        j = i % HEAD_GROUP
        base = (i // HEAD_GROUP) * 2
        return base if j < QK_NOPE else (base + 1 if j < QK_NOPE + QK_ROPE else -1)
    mq = _block_matrix(MXU_N, seg_q, lambda i: 1.0 / (QK_NOPE if i % HEAD_GROUP < QK_NOPE else QK_ROPE))
    mk = _block_matrix(MXU_N, lambda i: i // HEAD_GROUP if i % HEAD_GROUP < QK_NOPE else -1,
                       lambda i: 1.0 / QK_NOPE)
    m64sum = _block_matrix(MXU_N, lambda i: i // RWKV_HEAD, lambda i: 1.0)
    m64mean = _block_matrix(MXU_N, lambda i: i // RWKV_HEAD, lambda i: 1.0 / RWKV_HEAD)

    kvh = w_kv_up.reshape(KV_LORA, MLA_HEADS, QK_NOPE + V_DIM)
    wkk = pad_to(kvh[:, :, :QK_NOPE], HEAD_GROUP).reshape(KV_LORA, QK_COLS).astype(BF16)
    wv = kvh[:, :, QK_NOPE:].reshape(KV_LORA, MLA_DIM).astype(BF16)
    gkn = jnp.tile(pad_to(kn_nope_g, HEAD_GROUP), MLA_HEADS)
    erep_np = np.zeros((LANE, QK_COLS), np.float32)
    for h in range(MLA_HEADS):
        for j in range(QK_ROPE):
            erep_np[j, h * HEAD_GROUP + QK_NOPE + j] = 1.0
    erep = jnp.asarray(erep_np, BF16)
    gkr = pad_to(kn_rope_g, LANE)
    gkrs = pad_to(kn_rope_g[swap], LANE)

    w2e = jnp.concatenate([w2, jnp.zeros((AAA_LORA, RWKV_DIM), w2.dtype)], axis=0).astype(BF16)
    a2e = jnp.concatenate([jnp.zeros((DECAY_LORA, RWKV_DIM), a2.dtype), a2], axis=0).astype(BF16)

    inproj_w = [row(norm_mix_g), win, row(q_norm_g), wqa, wqb, row(gqa), row(gqb), mq,
                row(kv_norm_g), row(gkr), row(gkrs), wkk, row(gkn), mk, erep, wv, wv.T,
                row(mu_shift), row(w0), w2e, row(a0), a2e, row(k_k), row(k_a), m64sum]
    wkc = kvh[:, :, :QK_NOPE].reshape(KV_LORA, MLA_HEADS * QK_NOPE).astype(BF16)
    mct_np = np.zeros((16, MLA_HEADS * QK_NOPE), np.float32)
    for h in range(MLA_HEADS):
        mct_np[h, h * QK_NOPE:(h + 1) * QK_NOPE] = 1.0 / QK_NOPE
    sample_w = [wkc, row(kn_nope_g), jnp.asarray(mct_np, BF16), wv]
    return inproj_w, sample_w, m64mean, m64sum


def kernel(x_prompt, x_sample, cache_kv_latent, cache_k_rope, state_wkv, state_shift, meta_tokens, norm_mix_g, w_in, q_norm_g, w_q_up, kv_norm_g, w_kv_up, qn_nope_g, qn_rope_g, kn_nope_g, kn_rope_g, attn_out_g, mu_shift, w0, w2, a0, a2, g2, k_k, k_a, r_k, lnx_g, lnx_b, w_out, norm_ffn_g, w_gate, w_up, w_down):
    b, s, _ = x_prompt.shape
    db, ds, _ = x_sample.shape
    past = cache_kv_latent.shape[1]
    row = lambda v: v.astype(F32).reshape(1, -1)

    inproj_w, sample_w, m64mean, m64sum = _prep_weights(
        norm_mix_g, w_in, q_norm_g, w_q_up, kv_norm_g, w_kv_up, qn_nope_g, qn_rope_g,
        kn_nope_g, kn_rope_g, mu_shift, w0, w2, a0, a2, k_k, k_a)
    wkv_consts = (g2.astype(BF16), row(r_k), row(lnx_g), row(lnx_b), m64mean, m64sum)
    tail_w = [w_out.astype(BF16), row(norm_ffn_g), w_gate.astype(BF16), w_up.astype(BF16), w_down.astype(BF16)]
    g_attn = row(attn_out_g)

    ts = min(TOKEN_TILE, s)
    meta = _inproj(meta_tokens.astype(F32)[None], jnp.zeros((1, 1, RWKV_COLS), F32),
                   _rope_tables(jnp.arange(N_META)), inproj_w, N_META, values_transposed=True)
    c_m, kr_m, _, kc_m, vt_m = meta[:5]
    _, s_meta = _wkv(meta[5:12], jnp.zeros((1, RWKV_HEADS, RWKV_HEAD, RWKV_HEAD), F32),
                     wkv_consts, CHUNK, 1)

    pr = _inproj(x_prompt, meta[12], _rope_tables(N_META + jnp.arange(s)), inproj_w, ts, values_transposed=True,
                 lead_rows=(c_m[0], kr_m[0]))
    kv_latent_p, k_rope_p, q_p, kc_p, vt_p = pr[:5]
    rw_p, wkv_p = _wkv(pr[5:12], s_meta, wkv_consts, min(WKV_TILE, s), _streams(b, WKV_STREAMS))
    attn_p = _attn_prompt(q_p, kc_p, vt_p, kc_m[0], vt_m[0], g_attn, min(ATTN_TILE, s))
    y_prompt = _tail(x_prompt.reshape(b * s, D_MODEL), attn_p.reshape(b * s, MLA_DIM),
                     rw_p.reshape(b * s, RWKV_DIM), tail_w, ts).reshape(b, s, D_MODEL)

    sm = _inproj(x_sample, state_shift.astype(F32)[:, None, :], _rope_tables(past + jnp.arange(ds)),
                 inproj_w, ds, nb=_streams(db, SHORT_STREAMS))
    c_s, kr_s, q_s = sm[:3]
    rw_s, wkv_s = _wkv(sm[5:12], state_wkv.astype(F32), wkv_consts, CHUNK, _streams(db, SHORT_STREAMS))
    rw_s = rw_s[:, :ds]
    attn_s = _attn_sample(q_s, c_s, kr_s, cache_kv_latent, cache_k_rope, sample_w, g_attn,
                          min(SAMPLE_KEY_TILE, past))
    y_sample = _tail(x_sample.reshape(db * ds, D_MODEL), attn_s.reshape(db * ds, MLA_DIM),
                     rw_s.reshape(db * ds, RWKV_DIM), tail_w, min(TOKEN_TILE, db * ds)).reshape(db, ds, D_MODEL)

    return (y_prompt, y_sample, kv_latent_p, k_rope_p, wkv_p, pr[12][:, 0, :],
            c_s, kr_s, wkv_s, sm[12][:, 0, :])
```

```python
import functools

import numpy as np
import jax
import jax.numpy as jnp
from jax import lax
from jax.experimental import pallas as pl
from jax.experimental.pallas import tpu as pltpu

F32 = jnp.float32
BF16 = jnp.bfloat16

D_MODEL = 1024
CHUNK = 64
N_META = 16
EPS = 1e-6
NEG = -1e30
MLA_HEADS = 8
QK_NOPE = 64
QK_ROPE = 32
V_DIM = 64
Q_LORA = 256
KV_LORA = 128
ROPE_BASE = 10000.0
ATTN_SCALE = (QK_NOPE + QK_ROPE) ** -0.5
LOG2E = float(np.log2(np.e))
MLA_DIM = MLA_HEADS * V_DIM
RWKV_HEADS = 8
RWKV_HEAD = 64
RWKV_DIM = RWKV_HEADS * RWKV_HEAD
DECAY_LORA = 64
AAA_LORA = 64
GATE_LORA = 128
LNX_EPS = 64e-5
RWKV_COLS = 3 * RWKV_DIM + DECAY_LORA + AAA_LORA + GATE_LORA
D_FF = -(-8 * D_MODEL // (3 * 256)) * 256

LANE = 128
MXU_N = 256
HEAD_GROUP = LANE
QK_COLS = MLA_HEADS * HEAD_GROUP
MLA_EXT = 512
IN_EXT = MLA_EXT + RWKV_COLS
VMEM_LIMIT = 56 * 1024 * 1024

TOKEN_TILE = 512
ATTN_TILE = 512
SAMPLE_KEY_TILE = 4096
WKV_TILE = 4 * CHUNK
WKV_STREAMS = 2
SHORT_STREAMS = 4


def _streams(batch, want):
    return want if batch % want == 0 else 1


def _dot(a, b):
    return jnp.dot(a, b, preferred_element_type=F32)


def _dot_nt(a, b):
    return lax.dot_general(a, b, (((1,), (1,)), ((), ())), preferred_element_type=F32)


def _dot_tn(a, b):
    return lax.dot_general(a, b, (((0,), (0,)), ((), ())), preferred_element_type=F32)


def _seg_reduce(x, m, terms=2):
    parts = [x.astype(BF16)]
    if terms == 2:
        parts.append((x - parts[0].astype(F32)).astype(BF16))
    outs = []
    for j in range(x.shape[1] // MXU_N):
        sl = slice(j * MXU_N, (j + 1) * MXU_N)
        outs.append(sum(_dot(part[:, sl], m) for part in parts))
    return outs[0] if len(outs) == 1 else jnp.concatenate(outs, axis=1)


def _full_spec(arr):
    nd = arr.ndim
    return pl.BlockSpec(arr.shape, lambda *_: (0,) * nd)


def _params(sem):
    return pltpu.CompilerParams(dimension_semantics=sem, vmem_limit_bytes=VMEM_LIMIT)


def _inproj_kernel(x_ref, prev0_ref, cosq_ref, sinq_ref, cosk_ref, sink_ref,
                   gmix_ref, win_ref, gql_ref, wqa_ref, wqb_ref, gqa_ref, gqb_ref, mq_ref,
                   gkv_ref, gkr_ref, gkrs_ref, wkk_ref, gkn_ref, mk_ref, erep_ref, wv_ref, wvt_ref,
                   mu_ref, w0_ref, w2_ref, a0_ref, a2_ref, kkg_ref, ka_ref, m64_ref,
                   cm_ref, krm_ref,
                   c_ref, kr_ref, q_ref, kc_ref, vc_ref,
                   r_ref, lw_ref, k_ref, v_ref, kk_ref, b_ref, gl_ref, shift_ref,
                   carry_ref, *, values_transposed, lead):
    i = pl.program_id(1)
    nb, ts = x_ref.shape[:2]
    part = lambda val, bi: val[bi * ts:(bi + 1) * ts]

    def put(ref, val):
        for bi in range(nb):
            ref[bi] = part(val, bi)

    def put_rows(ref, val):
        if lead:
            ref[0, pl.ds(pl.multiple_of(lead + i * ts, 8), ts), :] = val
        else:
            put(ref, val)

    def put_padded(ref, val):
        if ref.shape[1] == ts:
            put(ref, val)
        else:
            for bi in range(nb):
                ref[bi, :ts, :] = part(val, bi)
                ref[bi, ts:, :] = jnp.zeros((ref.shape[1] - ts, val.shape[1]), val.dtype)

    if lead:
        @pl.when(i == 0)
        def _():
            c_ref[0, :lead, :] = cm_ref[...]
            kr_ref[0, :lead, :] = krm_ref[...]

    x = x_ref[0] if nb == 1 else jnp.concatenate([x_ref[bi] for bi in range(nb)], axis=0)
    tab = lambda ref: ref[...] if nb == 1 else jnp.concatenate([ref[...]] * nb, axis=0)
    xn = x * lax.rsqrt(jnp.mean(x * x, axis=-1, keepdims=True) + EPS) * gmix_ref[...]
    p = _dot(xn.astype(BF16), win_ref[...])

    pq = p[:, :Q_LORA]
    ql = pq * lax.rsqrt(jnp.mean(pq * pq, axis=-1, keepdims=True) + EPS) * gql_ref[...]
    ql = ql.astype(BF16)
    qa = _dot(ql, wqa_ref[...])
    qb = _dot(ql, wqb_ref[...])
    qs = lax.rsqrt(_seg_reduce(qa * qa, mq_ref[...], terms=1) + EPS)
    fa = tab(cosq_ref) * (gqa_ref[...] * (ATTN_SCALE * LOG2E))
    fb = tab(sinq_ref) * (gqb_ref[...] * (ATTN_SCALE * LOG2E))
    q = qs * (qa * jnp.concatenate([fa] * MLA_HEADS, axis=1) + qb * jnp.concatenate([fb] * MLA_HEADS, axis=1))
    put(q_ref, q.astype(BF16))

    pc = p[:, Q_LORA:Q_LORA + KV_LORA]
    c = pc * lax.rsqrt(jnp.mean(pc * pc, axis=-1, keepdims=True) + EPS) * gkv_ref[...]
    put_rows(c_ref, c)
    cb = c.astype(BF16)

    pk = p[:, Q_LORA + KV_LORA:MLA_EXT]
    lane = lax.broadcasted_iota(jnp.int32, pk.shape, 1)
    ssk = jnp.sum(jnp.where(lane < QK_ROPE, pk * pk, 0.0), axis=-1, keepdims=True)
    sk = lax.rsqrt(ssk * (1.0 / QK_ROPE) + EPS)
    pk_sw = pltpu.roll(pk, LANE - QK_ROPE, axis=1)
    kr = sk * (pk * gkr_ref[...] * tab(cosk_ref) + pk_sw * gkrs_ref[...] * tab(sink_ref))
    put_rows(kr_ref, kr[:, :QK_ROPE])

    kraw = _dot(cb, wkk_ref[...])
    kn = kraw * lax.rsqrt(_seg_reduce(kraw * kraw, mk_ref[...], terms=1) + EPS) * gkn_ref[...]
    put(kc_ref, (kn + _dot(kr.astype(BF16), erep_ref[...])).astype(BF16))
    if values_transposed:
        vc_ref[0] = _dot_nt(wvt_ref[...], cb).astype(BF16)
    else:
        put(vc_ref, _dot(cb, wv_ref[...]).astype(BF16))

    prw = p[:, MLA_EXT:]

    @pl.when(i == 0)
    def _():
        carry_ref[...] = jnp.broadcast_to(prev0_ref[...], carry_ref.shape)

    row = lax.broadcasted_iota(jnp.int32, (nb * ts, 1), 0)
    shifted = pltpu.roll(prw, 1, axis=0)
    for bi in range(nb):
        shifted = jnp.where(row == bi * ts, carry_ref[bi], shifted)
        last = prw[(bi + 1) * ts - 1:(bi + 1) * ts, :]
        carry_ref[bi] = last
        shift_ref[bi] = last
    xm = prw + (shifted - prw) * mu_ref[...]

    o = RWKV_DIM
    r = xm[:, :o]
    k = xm[:, o:2 * o]
    v = xm[:, 2 * o:3 * o]
    wa = xm[:, 3 * o:3 * o + DECAY_LORA + AAA_LORA]
    put_padded(gl_ref, xm[:, 3 * o + DECAY_LORA + AAA_LORA:])
    dw = _dot(jnp.tanh(wa).astype(BF16), w2_ref[...])
    da = _dot(wa.astype(BF16), a2_ref[...])
    put_padded(lw_ref, -jax.nn.sigmoid(w0_ref[...] + dw) * float(np.exp(-0.5)))
    a = jax.nn.sigmoid(a0_ref[...] + da)
    kx = k * kkg_ref[...]
    kk = kx * lax.rsqrt(jnp.maximum(_seg_reduce(kx * kx, m64_ref[...], terms=1), 1e-24))
    put_padded(r_ref, r)
    put_padded(k_ref, k * (1.0 + (a - 1.0) * ka_ref[...]))
    put_padded(v_ref, v)
    put_padded(kk_ref, kk)
    put_padded(b_ref, kk * a)


def _inproj(x, prev0, tabs, wts, ts, values_transposed=False, lead_rows=None, nb=1):
    b, t, _ = x.shape
    assert t % ts == 0 and b % nb == 0
    assert nb == 1 or not (values_transposed or lead_rows is not None)
    grid = (b // nb, t // ts)
    tok = lambda w: pl.BlockSpec((nb, ts, w), lambda bi, i: (bi, i, 0))
    lead = 0 if lead_rows is None else lead_rows[0].shape[0]
    if lead_rows is None:
        lead_rows = (jnp.zeros((8, KV_LORA), F32), jnp.zeros((8, QK_ROPE), F32))
    tpad = t if t % CHUNK == 0 else CHUNK
    assert tpad == t or t == ts
    in_specs = [tok(D_MODEL),
                pl.BlockSpec((nb if prev0.shape[0] > 1 else 1, 1, RWKV_COLS),
                             (lambda bi, i: (bi, 0, 0)) if prev0.shape[0] > 1 else (lambda bi, i: (0, 0, 0)))]
    in_specs += [pl.BlockSpec((ts, LANE), lambda bi, i: (i, 0)) for _ in tabs]
    in_specs += [_full_spec(w) for w in wts] + [_full_spec(w) for w in lead_rows]
    widths = [(KV_LORA, F32), (QK_ROPE, F32), (QK_COLS, BF16), (QK_COLS, BF16), (MLA_DIM, BF16)]
    widths += [(RWKV_DIM, F32)] * 6 + [(GATE_LORA, F32)]
    out_shape = [jax.ShapeDtypeStruct((b, t, w), dt) for w, dt in widths]
    out_specs = [tok(w) for w, _ in widths]
    if values_transposed:
        out_shape[4] = jax.ShapeDtypeStruct((b, MLA_DIM, t), BF16)
        out_specs[4] = pl.BlockSpec((1, MLA_DIM, ts), lambda bi, i: (bi, 0, i))
    if lead:
        for n in range(2):
            w, dt = widths[n]
            out_shape[n] = jax.ShapeDtypeStruct((b, lead + t, w), dt)
            out_specs[n] = pl.BlockSpec((1, lead + t, w), lambda bi, i: (bi, 0, 0))
    if tpad != t:
        for n in range(5, 12):
            w, dt = widths[n]
            out_shape[n] = jax.ShapeDtypeStruct((b, tpad, w), dt)
            out_specs[n] = pl.BlockSpec((nb, tpad, w), lambda bi, i: (bi, 0, 0))
    out_shape.append(jax.ShapeDtypeStruct((b, 1, RWKV_COLS), F32))
    out_specs.append(pl.BlockSpec((nb, 1, RWKV_COLS), lambda bi, i: (bi, 0, 0)))
    return pl.pallas_call(
        functools.partial(_inproj_kernel, values_transposed=values_transposed, lead=lead), grid=grid, in_specs=in_specs, out_specs=out_specs, out_shape=out_shape,
        scratch_shapes=[pltpu.VMEM((nb, 1, RWKV_COLS), F32)],
        compiler_params=_params(("arbitrary", "arbitrary")), name="inproj",
    )(x, prev0, *tabs, *wts, *lead_rows)


PAIRS = RWKV_HEADS // 2
PAIR_W = 2 * RWKV_HEAD


def _wkv_kernel(r_ref, lw_ref, k_ref, v_ref, kk_ref, b_ref, gl_ref, s0_ref,
                tri_ref, g2_ref, rk_ref, lng_ref, lnb_ref, m64mean_ref, m64sum_ref,
                rw_ref, sout_ref, state_ref):
    i = pl.program_id(1)
    nb, tb = r_ref.shape[:2]
    tt = nb * tb
    c = CHUNK
    nch = tt // c
    cps = tb // c

    @pl.when(i == 0)
    def _():
        state_ref[...] = jnp.broadcast_to(s0_ref[...], state_ref.shape)

    rows_of = lambda ref: ref[0] if nb == 1 else jnp.concatenate([ref[bi] for bi in range(nb)], axis=0)
    r = rows_of(r_ref)
    lw = rows_of(lw_ref)
    k = rows_of(k_ref)
    v = rows_of(v_ref)
    tri = tri_ref[...]
    l1 = lw.astype(BF16)
    e1 = lw - l1.astype(F32)
    l2 = e1.astype(BF16)
    l3 = (e1 - l2.astype(F32)).astype(BF16)
    psum = lambda x: _dot(tri, x[:tb]) if nb == 1 else jnp.concatenate(
        [_dot(tri, x[bi * tb:(bi + 1) * tb]) for bi in range(nb)], axis=0)
    cum = psum(l1) + psum(l2) + psum(l3)
    g = jnp.exp(cum)
    gi = jnp.exp(-cum)
    gp = jnp.exp(cum - lw)
    gc_rows = [g[j * c + c - 1:(j + 1) * c, :] for j in range(nch)]
    gcb = jnp.concatenate([jnp.broadcast_to(x, (c, RWKV_DIM)) for x in gc_rows], axis=0)
    at = -(rows_of(kk_ref) * gp)
    bt = rows_of(b_ref) * gi
    kt = k * gi
    rt = r * g

    even = (lax.broadcasted_iota(jnp.int32, (tt, RWKV_DIM), 1) % PAIR_W) < RWKV_HEAD
    split = lambda x: (jnp.where(even, x, 0.0).astype(BF16), jnp.where(even, 0.0, x).astype(BF16))
    atm_b, rtm_b, vm_b = split(at), split(rt), split(v)
    bt_b, kt_b, v_b = bt.astype(BF16), kt.astype(BF16), v.astype(BF16)
    btc_b, ktc_b = (bt * gcb).astype(BF16), (kt * gcb).astype(BF16)

    lane = lax.broadcasted_iota(jnp.int32, (c, PAIR_W), 1)
    lo = lane < c
    eye_hi = (lane == lax.broadcasted_iota(jnp.int32, (c, PAIR_W), 0) + c).astype(F32)
    rowi = lax.broadcasted_iota(jnp.int32, (2 * c, 2 * c), 0)
    coli = lax.broadcasted_iota(jnp.int32, (2 * c, 2 * c), 1)
    colt = jnp.where(coli >= c, coli - c, coli)
    gmask = colt < jnp.where(rowi < c, rowi, rowi - c + 1)
    blk = (rowi // c) == (coli // c)
    z1 = jnp.zeros((c, PAIR_W), BF16)
    z2 = jnp.zeros((c, 2 * PAIR_W), BF16)

    items = [(j, h) for j in range(nch) for h in range(RWKV_HEADS)]
    rows = lambda j: slice(j * c, (j + 1) * c)
    lanes = lambda h: slice((h // 2) * PAIR_W, (h // 2 + 1) * PAIR_W)
    cut = lambda x, j, h: x[rows(j), lanes(h)]
    vstack = lambda a, b: jnp.concatenate([a, b], axis=0)
    hstack = lambda a, b: jnp.concatenate([a, b], axis=1)

    gb, xs = [], []
    for j, h in items:
        ar = vstack(cut(atm_b[h % 2], j, h), cut(rtm_b[h % 2], j, h))
        bk = vstack(cut(bt_b, j, h), cut(kt_b, j, h))
        gm = jnp.where(gmask, _dot_nt(ar, bk), 0.0)
        gb.append(gm.astype(BF16))
        xs.append(jnp.where(lo, gm[:c], eye_hi))
    for _ in range(int(np.log2(c)) - 1):
        xs = [_dot(x[:, :c].astype(BF16), x.astype(BF16)) + jnp.where(lo, 0.0, x) for x in xs]
    xb = [x.astype(BF16) for x in xs]
    vm = [cut(vm_b[h % 2], j, h) for j, h in items]
    akv = [_dot(gb[n][:c], vstack(z1, vm[n])) for n in range(len(items))]
    zc = [hstack(cut(atm_b[h % 2], j, h), akv[n].astype(BF16)) for n, (j, h) in enumerate(items)]
    w = [_dot(xb[n], vstack(z2, zc[n])) for n in range(len(items))]
    tu = [w[n] + _dot(xb[n], vstack(w[n].astype(BF16), z2)) for n in range(len(items))]
    ry = [_dot(gb[n][c:], vstack(tu[n].astype(BF16), hstack(z1, vm[n]))) for n in range(len(items))]

    reff, yv, pm, qm = {}, {}, {}, {}
    for j in range(nch):
        for p in range(PAIRS):
            n0 = j * RWKV_HEADS + 2 * p
            tu_p = tu[n0] + tu[n0 + 1]
            ry_p = ry[n0] + ry[n0 + 1]
            h = 2 * p
            reff[j, p] = (cut(rt, j, h) + ry_p[:, :PAIR_W]).astype(BF16)
            yv[j, p] = ry_p[:, PAIR_W:]
            pm[j, p] = jnp.where(blk, _dot_tn(tu_p[:, :PAIR_W].astype(BF16), cut(btc_b, j, h)), 0.0).astype(BF16)
            qm[j, p] = jnp.where(blk, _dot_tn(vstack(tu_p[:, PAIR_W:].astype(BF16), cut(v_b, j, h)),
                                              vstack(cut(btc_b, j, h), cut(ktc_b, j, h))), 0.0)

    st = [[state_ref[bi, p] for p in range(PAIRS)] for bi in range(nb)]
    ys = [None] * nch
    for jj in range(cps):
        for bi in range(nb):
            j = bi * cps + jj
            sb = [s.astype(BF16) for s in st[bi]]
            ys[j] = jnp.concatenate([_dot_nt(reff[j, p], sb[p]) + yv[j, p] for p in range(PAIRS)], axis=1)
            st[bi] = [st[bi][p] * gc_rows[j][:, p * PAIR_W:(p + 1) * PAIR_W] + _dot(sb[p], pm[j, p]) + qm[j, p]
                      for p in range(PAIRS)]
    for bi in range(nb):
        for p in range(PAIRS):
            state_ref[bi, p] = st[bi][p]
    sout_ref[...] = state_ref[...]

    y = ys[0] if nch == 1 else jnp.concatenate(ys, axis=0)
    mean = _seg_reduce(y, m64mean_ref[...])
    d = y - mean
    var = _seg_reduce(d * d, m64mean_ref[...], terms=1)
    yn = d * lax.rsqrt(var + LNX_EPS) * lng_ref[...] + lnb_ref[...]
    bonus = _seg_reduce(r * k * rk_ref[...], m64sum_ref[...], terms=1) * v
    gate = _dot(jax.nn.sigmoid(rows_of(gl_ref)).astype(BF16), g2_ref[...])
    rw = ((yn + bonus) * gate).astype(rw_ref.dtype)
    for bi in range(nb):
        rw_ref[bi] = rw[bi * tb:(bi + 1) * tb]


def _state_to_pairs(s):
    z = jnp.zeros_like(s[:, 0::2])
    top = jnp.concatenate([s[:, 0::2], z], axis=-1)
    bot = jnp.concatenate([z, s[:, 1::2]], axis=-1)
    return jnp.concatenate([top, bot], axis=2)


def _pairs_to_state(sp):
    b = sp.shape[0]
    s = jnp.stack([sp[:, :, :RWKV_HEAD, :RWKV_HEAD], sp[:, :, RWKV_HEAD:, RWKV_HEAD:]], axis=2)
    return s.reshape(b, RWKV_HEADS, RWKV_HEAD, RWKV_HEAD)


def _wkv(seqs, s0, consts, tt, nb):
    b, t, _ = seqs[0].shape
    assert b % nb == 0 and t % tt == 0 and tt % CHUNK == 0
    tok = lambda w: pl.BlockSpec((nb, tt, w), lambda bi, i: (bi, i, 0))
    st_shape = (nb, PAIRS, PAIR_W, PAIR_W)
    in_specs = [tok(RWKV_DIM)] * 6 + [tok(GATE_LORA)]
    s0_shape = (1,) + st_shape[1:] if s0.shape[0] == 1 else st_shape
    in_specs.append(pl.BlockSpec(s0_shape, (lambda bi, i: (bi, 0, 0, 0)) if s0.shape[0] > 1 else (lambda bi, i: (0, 0, 0, 0))))
    tri = jnp.kron(jnp.eye(tt // CHUNK, dtype=F32), jnp.tril(jnp.ones((CHUNK, CHUNK), F32))).astype(BF16)
    consts = (tri,) + tuple(consts)
    in_specs += [_full_spec(w) for w in consts]
    rw, sp = pl.pallas_call(
        _wkv_kernel, grid=(b // nb, t // tt),
        in_specs=in_specs,
        out_specs=[tok(RWKV_DIM), pl.BlockSpec(st_shape, lambda bi, i: (bi, 0, 0, 0))],
        out_shape=[jax.ShapeDtypeStruct((b, t, RWKV_DIM), BF16),
                   jax.ShapeDtypeStruct((b,) + st_shape[1:], F32)],
        scratch_shapes=[pltpu.VMEM(st_shape, F32)],
        compiler_params=_params(("arbitrary", "arbitrary")), name="wkv",
    )(*seqs, _state_to_pairs(s0), *consts)
    return rw, _pairs_to_state(sp)


_hq = lambda h: slice(h * HEAD_GROUP, (h + 1) * HEAD_GROUP)


L_ROWS = 16
ACC_ROWS = V_DIM + L_ROWS


def _attn_step_t(qh, kt, vt, mask, first, m_ref, acc_ref, qw=slice(None)):
    n = vt(0).shape[1]
    ones = jnp.ones((L_ROWS, n), BF16)
    s = [_dot_nt(kt(h), qh(h)) for h in range(MLA_HEADS)]
    if mask is not None:
        s = [jnp.where(mask, x, NEG) for x in s]
    ps, alphas = [], []
    for h in range(MLA_HEADS):
        mx = jnp.max(s[h], axis=0, keepdims=True)
        if first:
            m_new = mx
        else:
            m_old = m_ref[h, :, qw]
            m_new = jnp.maximum(m_old, mx)
            alphas.append(jnp.exp2(m_old - m_new))
        m_ref[h, :, qw] = m_new
        ps.append(jnp.exp2(s[h] - m_new).astype(BF16))
    for h in range(MLA_HEADS):
        pv = _dot(jnp.concatenate([vt(h), ones], axis=0), ps[h])
        acc_ref[h, :, qw] = pv if first else alphas[h] * acc_ref[h, :, qw] + pv


def _attn_prompt_kernel(q_ref, k_ref, vt_ref, km_ref, vtm_ref, g_ref, o_ref, m_ref, acc_ref, *, tk):
    i = pl.program_id(1)
    tq = q_ref.shape[1]
    assert tq == tk and tq % (2 * CHUNK) == 0
    half = tq // 2
    st = (m_ref, acc_ref)
    qh = lambda h: q_ref[0, :, _hq(h)]
    hd = lambda h: slice(h * V_DIM, (h + 1) * V_DIM)

    k0 = pl.ds(pl.multiple_of(i * tk, tk), half)
    krow = lax.broadcasted_iota(jnp.int32, (half + N_META, tq), 0) // CHUNK
    qcol = lax.broadcasted_iota(jnp.int32, (half + N_META, tq), 1) // CHUNK
    mask0 = (krow <= qcol) | (lax.broadcasted_iota(jnp.int32, (half + N_META, tq), 0) >= half)
    _attn_step_t(qh, lambda h: jnp.concatenate([k_ref[0, k0, _hq(h)], km_ref[:, _hq(h)]], axis=0),
                 lambda h: jnp.concatenate([vt_ref[0, hd(h), k0], vtm_ref[hd(h), :]], axis=1), mask0, True, *st)

    def body(t, carry):
        ks = pl.ds(pl.multiple_of(t * tk, tk), tk)
        _attn_step_t(qh, lambda h: k_ref[0, ks, _hq(h)], lambda h: vt_ref[0, hd(h), ks], None, False, *st)
        return carry

    lax.fori_loop(0, i, body, 0)
    k1 = pl.ds(pl.multiple_of(i * tk + half, half), half)
    krow = lax.broadcasted_iota(jnp.int32, (half, tq), 0) // CHUNK
    qcol = lax.broadcasted_iota(jnp.int32, (half, tq), 1) // CHUNK
    _attn_step_t(lambda h: q_ref[0, half:, _hq(h)], lambda h: k_ref[0, k1, _hq(h)],
                 lambda h: vt_ref[0, hd(h), k1], (krow <= qcol)[:, :half], False, *st, qw=slice(half, tq))

    o_t = jnp.concatenate([acc_ref[h, :V_DIM, :] / acc_ref[h, V_DIM:V_DIM + 1, :] for h in range(MLA_HEADS)],
                          axis=0)
    o_t = o_t * lax.rsqrt(jnp.mean(o_t * o_t, axis=0, keepdims=True) + EPS)
    o_ref[0] = (o_t.T * g_ref[...]).astype(o_ref.dtype)


def _attn_prompt(q, kc, vt, kmeta, vtmeta, g, tq):
    b, s, _ = q.shape
    assert s % tq == 0
    return pl.pallas_call(
        functools.partial(_attn_prompt_kernel, tk=tq), grid=(b, s // tq),
        in_specs=[pl.BlockSpec((1, tq, QK_COLS), lambda bi, i: (bi, i, 0)),
                  pl.BlockSpec((1, s, QK_COLS), lambda bi, i: (bi, 0, 0)),
                  pl.BlockSpec((1, MLA_DIM, s), lambda bi, i: (bi, 0, 0)),
                  _full_spec(kmeta), _full_spec(vtmeta), _full_spec(g)],
        out_specs=pl.BlockSpec((1, tq, MLA_DIM), lambda bi, i: (bi, i, 0)),
        out_shape=jax.ShapeDtypeStruct((b, s, MLA_DIM), BF16),
        scratch_shapes=[pltpu.VMEM((MLA_HEADS, 1, tq), F32), pltpu.VMEM((MLA_HEADS, ACC_ROWS, tq), F32)],
        compiler_params=_params(("arbitrary", "arbitrary")), name="attn_prompt",
    )(q, kc, vt, kmeta, vtmeta, g)


def _attn_sample_kernel(q_ref, qr_ref, cn_ref, krn_ref, cc_ref, ck_ref, wkc_ref, gk_ref, mct_ref, wv_ref,
                        g_ref, o_ref, m_ref, l_ref, acc_ref, *, tk, past):
    tq = q_ref.shape[1]
    rows = MLA_HEADS * tq
    qp = jnp.concatenate(
        [_dot_nt((q_ref[0, :, h * HEAD_GROUP:h * HEAD_GROUP + QK_NOPE].astype(F32) * gk_ref[...]).astype(BF16),
                 wkc_ref[:, h * QK_NOPE:(h + 1) * QK_NOPE]) for h in range(MLA_HEADS)], axis=0).astype(BF16)
    qr = qr_ref[0]
    qchunk = (past + lax.broadcasted_iota(jnp.int32, (rows, 1), 0) % tq) // CHUNK

    def step(cb, krb, kpos, first):
        n = cb.shape[0]
        kraw = _dot(cb, wkc_ref[...])
        ss_t = _dot_nt(mct_ref[...], (kraw * kraw).astype(BF16))
        sc_t = lax.rsqrt(ss_t + EPS)
        scale = jnp.concatenate([jnp.broadcast_to(sc_t[h:h + 1, :], (tq, n)) for h in range(MLA_HEADS)], axis=0)
        s = _dot_nt(qp, cb) * scale + _dot_nt(qr, krb)
        kchunk = (kpos + lax.broadcasted_iota(jnp.int32, (1, n), 1)) // CHUNK
        s = jnp.where(kchunk <= qchunk, s, NEG)
        mx = jnp.max(s, axis=-1, keepdims=True)
        if first:
            m_new = mx
        else:
            m_old = m_ref[...]
            m_new = jnp.maximum(m_old, mx)
            alpha = jnp.exp2(m_old - m_new)
        pr = jnp.exp2(s - m_new)
        sm = jnp.sum(pr, axis=-1, keepdims=True)
        pc = _dot(pr.astype(BF16), cb)
        m_ref[...] = m_new
        l_ref[...] = sm if first else alpha * l_ref[...] + sm
        acc_ref[...] = pc if first else alpha * acc_ref[...] + pc

    step(cn_ref[0].astype(BF16), krn_ref[0].astype(BF16), past, True)

    def body(t, carry):
        ks = pl.ds(pl.multiple_of(t * tk, tk), tk)
        step(cc_ref[0, ks, :].astype(BF16), ck_ref[0, ks, :].astype(BF16), t * tk, False)
        return carry

    lax.fori_loop(0, past // tk, body, 0)
    oc = (acc_ref[...] / l_ref[...]).astype(BF16)
    oall = _dot(oc, wv_ref[...])
    lane_head = lax.broadcasted_iota(jnp.int32, (tq, MLA_DIM), 1) // V_DIM
    o = jnp.zeros((tq, MLA_DIM), F32)
    for h in range(MLA_HEADS):
        o = jnp.where(lane_head == h, oall[h * tq:(h + 1) * tq], o)
    o = o * lax.rsqrt(jnp.mean(o * o, axis=-1, keepdims=True) + EPS) * g_ref[...]
    o_ref[0] = o.astype(o_ref.dtype)


def _attn_sample(q, c_new, kr_new, cache_c, cache_kr, wts, g, tk):
    b, t, _ = q.shape
    past = cache_c.shape[1]
    assert past % tk == 0
    rows = MLA_HEADS * t
    bspec =lambda shp: pl.BlockSpec((1,) + shp, lambda bi: (bi, 0, 0))
    qr = q.reshape(b, t, MLA_HEADS, HEAD_GROUP)[..., QK_NOPE:QK_NOPE + QK_ROPE]
    qr = jnp.swapaxes(qr, 1, 2).reshape(b, rows, QK_ROPE)
    return pl.pallas_call(
        functools.partial(_attn_sample_kernel, tk=tk, past=past), grid=(b,),
        in_specs=[bspec((t, QK_COLS)), bspec((rows, QK_ROPE)), bspec((t, KV_LORA)), bspec((t, QK_ROPE)),
                  bspec((past, KV_LORA)), bspec((past, QK_ROPE))]
                 + [_full_spec(w) for w in wts] + [_full_spec(g)],
        out_specs=bspec((t, MLA_DIM)),
        out_shape=jax.ShapeDtypeStruct((b, t, MLA_DIM), BF16),
        scratch_shapes=[pltpu.VMEM((rows, 1), F32), pltpu.VMEM((rows, 1), F32), pltpu.VMEM((rows, KV_LORA), F32)],
        compiler_params=_params(("arbitrary",)), name="attn_sample",
    )(q, qr, c_new, kr_new, cache_c, cache_kr, *wts, g)


def _tail_kernel(x_ref, at_ref, rw_ref, wo_ref, gf_ref, wg_ref, wu_ref, wd_ref, o_ref, *, fc):
    mix = jnp.concatenate([at_ref[...], rw_ref[...]], axis=1)
    h = x_ref[...] + _dot(mix, wo_ref[...])
    u = (h * lax.rsqrt(jnp.mean(h * h, axis=-1, keepdims=True) + EPS) * gf_ref[...]).astype(BF16)
    acc = h
    for j in range(D_FF // fc):
        cs = slice(j * fc, (j + 1) * fc)
        gt = _dot(u, wg_ref[:, cs])
        up = _dot(u, wu_ref[:, cs])
        act = (gt * jax.nn.sigmoid(gt) * up).astype(BF16)
        acc = acc + _dot(act, wd_ref[cs, :])
    o_ref[...] = acc


def _tail(x, attn, rw, wts, ts):
    n = x.shape[0]
    assert n % ts == 0
    tok = lambda w: pl.BlockSpec((ts, w), lambda i: (i, 0))
    wspec = lambda w: pl.BlockSpec(w.shape, lambda i: (0,) * w.ndim, pipeline_mode=pl.Buffered(1))
    return pl.pallas_call(
        functools.partial(_tail_kernel, fc=MXU_N), grid=(n // ts,),
        in_specs=[tok(D_MODEL), tok(MLA_DIM), tok(RWKV_DIM)] + [wspec(w) for w in wts],
        out_specs=tok(D_MODEL),
        out_shape=jax.ShapeDtypeStruct((n, D_MODEL), F32),
        compiler_params=_params(("arbitrary",)), name="tail",
    )(x, attn, rw, *wts)


def _block_matrix(size, seg_of, scale_of):
    seg = np.array([seg_of(i) for i in range(size)])
    m = np.zeros((size, size), np.float32)
    for i in range(size):
        if seg[i] >= 0:
            m[i, seg == seg[i]] = scale_of(i)
    return jnp.asarray(m, BF16)


def _rope_tables(pos):
    half = QK_ROPE // 2
    inv = ROPE_BASE ** (-jnp.arange(half, dtype=F32) / half)
    ang = pos.astype(F32)[:, None] * inv[None, :]
    cos, sin = jnp.cos(ang), jnp.sin(ang)
    t = pos.shape[0]
    cc = jnp.concatenate([cos, cos], axis=1)
    ss = jnp.concatenate([-sin, sin], axis=1)
    zq = jnp.zeros((t, LANE - QK_NOPE - QK_ROPE), F32)
    cosq = jnp.concatenate([jnp.ones((t, QK_NOPE), F32), cc, zq], axis=1)
    sinq = jnp.concatenate([jnp.zeros((t, QK_NOPE), F32), ss, zq], axis=1)
    zk = jnp.zeros((t, LANE - QK_ROPE), F32)
    return cosq, sinq, jnp.concatenate([cc, zk], axis=1), jnp.concatenate([ss, zk], axis=1)


def _prep_weights(norm_mix_g, w_in, q_norm_g, w_q_up, kv_norm_g, w_kv_up, qn_nope_g, qn_rope_g,
                  kn_nope_g, kn_rope_g, mu_shift, w0, w2, a0, a2, k_k, k_a):
    half = QK_ROPE // 2
    swap = np.concatenate([np.arange(half, QK_ROPE), np.arange(half)])
    row = lambda v: v.astype(F32).reshape(1, -1)
    pad_to = lambda v, n: jnp.concatenate([v, jnp.zeros(v.shape[:-1] + (n - v.shape[-1],), v.dtype)], axis=-1)

    mla_cols = Q_LORA + KV_LORA + QK_ROPE
    w_kr = w_in[:, Q_LORA + KV_LORA:mla_cols]
    win = jnp.concatenate([pad_to(jnp.concatenate([w_in[:, :mla_cols], w_kr[:, swap]], axis=1), MLA_EXT),
                           w_in[:, mla_cols:]], axis=1).astype(BF16)

    qh = w_q_up.reshape(Q_LORA, MLA_HEADS, QK_NOPE + QK_ROPE)
    zq = jnp.zeros((Q_LORA, MLA_HEADS, QK_NOPE), w_q_up.dtype)
    wqa = pad_to(qh, HEAD_GROUP).reshape(Q_LORA, QK_COLS).astype(BF16)
    wqb = pad_to(jnp.concatenate([zq, qh[:, :, QK_NOPE:][:, :, swap]], axis=2), HEAD_GROUP)
    wqb = wqb.reshape(Q_LORA, QK_COLS).astype(BF16)
    gqa = pad_to(jnp.concatenate([qn_nope_g, qn_rope_g]), HEAD_GROUP)
    gqb = pad_to(jnp.concatenate([jnp.zeros_like(qn_nope_g), qn_rope_g[swap]]), HEAD_GROUP)

    def seg_q(i):
        j = i % HEAD_GROUP
        base = (i // HEAD_GROUP) * 2
        return base if j < QK_NOPE else (base + 1 if j < QK_NOPE + QK_ROPE else -1)
    mq = _block_matrix(MXU_N, seg_q, lambda i: 1.0 / (QK_NOPE if i % HEAD_GROUP < QK_NOPE else QK_ROPE))
    mk = _block_matrix(MXU_N, lambda i: i // HEAD_GROUP if i % HEAD_GROUP < QK_NOPE else -1,
                       lambda i: 1.0 / QK_NOPE)
    m64sum = _block_matrix(MXU_N, lambda i: i // RWKV_HEAD, lambda i: 1.0)
    m64mean = _block_matrix(MXU_N, lambda i: i // RWKV_HEAD, lambda i: 1.0 / RWKV_HEAD)

    kvh = w_kv_up.reshape(KV_LORA, MLA_HEADS, QK_NOPE + V_DIM)
    wkk = pad_to(kvh[:, :, :QK_NOPE], HEAD_GROUP).reshape(KV_LORA, QK_COLS).astype(BF16)
    wv = kvh[:, :, QK_NOPE:].reshape(KV_LORA, MLA_DIM).astype(BF16)
    gkn = jnp.tile(pad_to(kn_nope_g, HEAD_GROUP), MLA_HEADS)
    erep_np = np.zeros((LANE, QK_COLS), np.float32)
    for h in range(MLA_HEADS):
        for j in range(QK_ROPE):
            erep_np[j, h * HEAD_GROUP + QK_NOPE + j] = 1.0
    erep = jnp.asarray(erep_np, BF16)
    gkr = pad_to(kn_rope_g, LANE)
    gkrs = pad_to(kn_rope_g[swap], LANE)

    w2e = jnp.concatenate([w2, jnp.zeros((AAA_LORA, RWKV_DIM), w2.dtype)], axis=0).astype(BF16)
    a2e = jnp.concatenate([jnp.zeros((DECAY_LORA, RWKV_DIM), a2.dtype), a2], axis=0).astype(BF16)

    inproj_w = [row(norm_mix_g), win, row(q_norm_g), wqa, wqb, row(gqa), row(gqb), mq,
                row(kv_norm_g), row(gkr), row(gkrs), wkk, row(gkn), mk, erep, wv, wv.T,
                row(mu_shift), row(w0), w2e, row(a0), a2e, row(k_k), row(k_a), m64sum]
    wkc = kvh[:, :, :QK_NOPE].reshape(KV_LORA, MLA_HEADS * QK_NOPE).astype(BF16)
    mct_np = np.zeros((16, MLA_HEADS * QK_NOPE), np.float32)
    for h in range(MLA_HEADS):
        mct_np[h, h * QK_NOPE:(h + 1) * QK_NOPE] = 1.0 / QK_NOPE
    sample_w = [wkc, row(kn_nope_g), jnp.asarray(mct_np, BF16), wv]
    return inproj_w, sample_w, m64mean, m64sum


def kernel(x_prompt, x_sample, cache_kv_latent, cache_k_rope, state_wkv, state_shift, meta_tokens, norm_mix_g, w_in, q_norm_g, w_q_up, kv_norm_g, w_kv_up, qn_nope_g, qn_rope_g, kn_nope_g, kn_rope_g, attn_out_g, mu_shift, w0, w2, a0, a2, g2, k_k, k_a, r_k, lnx_g, lnx_b, w_out, norm_ffn_g, w_gate, w_up, w_down):
    b, s, _ = x_prompt.shape
    db, ds, _ = x_sample.shape
    past = cache_kv_latent.shape[1]
    row = lambda v: v.astype(F32).reshape(1, -1)

    inproj_w, sample_w, m64mean, m64sum = _prep_weights(
        norm_mix_g, w_in, q_norm_g, w_q_up, kv_norm_g, w_kv_up, qn_nope_g, qn_rope_g,
        kn_nope_g, kn_rope_g, mu_shift, w0, w2, a0, a2, k_k, k_a)
    wkv_consts = (g2.astype(BF16), row(r_k), row(lnx_g), row(lnx_b), m64mean, m64sum)
    tail_w = [w_out.astype(BF16), row(norm_ffn_g), w_gate.astype(BF16), w_up.astype(BF16), w_down.astype(BF16)]
    g_attn = row(attn_out_g)

    ts = min(TOKEN_TILE, s)
    meta = _inproj(meta_tokens.astype(F32)[None], jnp.zeros((1, 1, RWKV_COLS), F32),
                   _rope_tables(jnp.arange(N_META)), inproj_w, N_META, values_transposed=True)
    c_m, kr_m, _, kc_m, vt_m = meta[:5]
    _, s_meta = _wkv(meta[5:12], jnp.zeros((1, RWKV_HEADS, RWKV_HEAD, RWKV_HEAD), F32),
                     wkv_consts, CHUNK, 1)

    pr = _inproj(x_prompt, meta[12], _rope_tables(N_META + jnp.arange(s)), inproj_w, ts, values_transposed=True,
                 lead_rows=(c_m[0], kr_m[0]))
    kv_latent_p, k_rope_p, q_p, kc_p, vt_p = pr[:5]
    rw_p, wkv_p = _wkv(pr[5:12], s_meta, wkv_consts, min(WKV_TILE, s), _streams(b, WKV_STREAMS))
    attn_p = _attn_prompt(q_p, kc_p, vt_p, kc_m[0], vt_m[0], g_attn, min(ATTN_TILE, s))
    y_prompt = _tail(x_prompt.reshape(b * s, D_MODEL), attn_p.reshape(b * s, MLA_DIM),
                     rw_p.reshape(b * s, RWKV_DIM), tail_w, ts).reshape(b, s, D_MODEL)

    sm = _inproj(x_sample, state_shift.astype(F32)[:, None, :], _rope_tables(past + jnp.arange(ds)),
                 inproj_w, ds, nb=_streams(db, SHORT_STREAMS))
    c_s, kr_s, q_s = sm[:3]
    rw_s, wkv_s = _wkv(sm[5:12], state_wkv.astype(F32), wkv_consts, CHUNK, _streams(db, SHORT_STREAMS))
    rw_s = rw_s[:, :ds]
    attn_s = _attn_sample(q_s, c_s, kr_s, cache_kv_latent, cache_k_rope, sample_w, g_attn,
                          min(SAMPLE_KEY_TILE, past))
    y_sample = _tail(x_sample.reshape(db * ds, D_MODEL), attn_s.reshape(db * ds, MLA_DIM),
                     rw_s.reshape(db * ds, RWKV_DIM), tail_w, min(TOKEN_TILE, db * ds)).reshape(db, ds, D_MODEL)

    return (y_prompt, y_sample, kv_latent_p, k_rope_p, wkv_p, pr[12][:, 0, :],
            c_s, kr_s, wkv_s, sm[12][:, 0, :])
```
